```python
import math
import jax, jax.numpy as jnp
from jax import lax
import numpy as np

D_MODEL = 1024
BATCH = 2
SEQ = 16384
DEPTH = 2
DEC_BATCH = 16
DEC_SEQ = 64
PAST_LEN = 2048

CHUNK = 64
N_MEM = 256
CONV_CH = D_MODEL
CONV_WIDTH = 31
N_Q = 16
N_KV = 2
HEAD_DIM = 64
G = N_Q // N_KV
WINDOW = 128
WIN_CHUNKS = WINDOW // CHUNK
ROPE_THETA = 10000.0
N_XH = 4
XHEAD_DIM = 128
X_WIDTH = N_XH * XHEAD_DIM
N_CONV_LAYERS = (DEPTH + 1) // 2
N_ATTN_LAYERS = DEPTH // 2
ATTN_Q = N_Q * HEAD_DIM
ATTN_KV = N_KV * HEAD_DIM
CONV_IN = 3 * CONV_CH + 2 * X_WIDTH
ATTN_IN = 2 * ATTN_Q + 2 * ATTN_KV + 2 * X_WIDTH
BRANCH_W = CONV_CH + X_WIDTH
RMS_EPS = 1e-6
LN_EPS = 1e-5

kernel_name = 'streaming_conv_swa_mem_hybrid'


def rms_norm(x, g):
    xf = x.astype(jnp.float32)
    y = xf * lax.rsqrt(jnp.mean(xf * xf, axis=-1, keepdims=True) + RMS_EPS)
    return (y * g.astype(jnp.float32)).astype(x.dtype)


def layer_norm(x, g, b):
    xf = x.astype(jnp.float32)
    mu = jnp.mean(xf, axis=-1, keepdims=True)
    xc = xf - mu
    y = xc * lax.rsqrt(jnp.mean(xc * xc, axis=-1, keepdims=True) + LN_EPS)
    return (y * g.astype(jnp.float32) + b.astype(jnp.float32)).astype(x.dtype)


def rope(x, pos):
    half = HEAD_DIM // 2
    inv = ROPE_THETA ** (-jnp.arange(half, dtype=jnp.float32) / half)
    ang = pos.astype(jnp.float32)[:, None] * inv[None, :]
    cos = jnp.cos(ang)[:, None, :]
    sin = jnp.sin(ang)[:, None, :]
    xf = x.astype(jnp.float32)
    x1, x2 = xf[..., :half], xf[..., half:]
    return jnp.concatenate([x1 * cos - x2 * sin, x2 * cos + x1 * sin], axis=-1).astype(x.dtype)


def sink_softmax(s, sink):
    sink = jnp.broadcast_to(sink.astype(jnp.float32), s.shape[:-1] + (1,))
    return jax.nn.softmax(jnp.concatenate([s, sink], axis=-1), axis=-1)[..., :-1]


def swa_banded(q, k, v, sinks):
    B, S = q.shape[:2]
    nc = S // CHUNK
    pad = WIN_CHUNKS * CHUNK
    L = (WIN_CHUNKS + 1) * CHUNK
    kp = jnp.pad(k, ((0, 0), (pad, 0), (0, 0), (0, 0))).reshape(B, nc + WIN_CHUNKS, CHUNK, N_KV, HEAD_DIM)
    vp = jnp.pad(v, ((0, 0), (pad, 0), (0, 0), (0, 0))).reshape(B, nc + WIN_CHUNKS, CHUNK, N_KV, HEAD_DIM)
    kb = jnp.concatenate([kp[:, j:j + nc] for j in range(WIN_CHUNKS + 1)], axis=2)
    vb = jnp.concatenate([vp[:, j:j + nc] for j in range(WIN_CHUNKS + 1)], axis=2)
    qb = q.reshape(B, nc, CHUNK, N_KV, G, HEAD_DIM)
    s = jnp.einsum('bcqkgd,bcjkd->bckgqj', qb, kb).astype(jnp.float32) * (1.0 / math.sqrt(HEAD_DIM))
    key_chunk = jnp.arange(nc)[:, None] - WIN_CHUNKS + jnp.arange(L)[None, :] // CHUNK
    valid = (key_chunk >= 0)[None, :, None, None, None, :]
    s = jnp.where(valid, s, -jnp.inf)
    p = sink_softmax(s, sinks.reshape(1, 1, N_KV, G, 1, 1))
    o = jnp.einsum('bckgqj,bcjkd->bcqkgd', p.astype(vb.dtype), vb)
    return o.reshape(B, S, ATTN_Q)


def swa_with_past(q, kk, vv, sinks):
    B, T = q.shape[:2]
    qb = q.reshape(B, T, N_KV, G, HEAD_DIM)
    s = jnp.einsum('btkgd,bjkd->bkgtj', qb, kk).astype(jnp.float32) * (1.0 / math.sqrt(HEAD_DIM))
    p = sink_softmax(s, sinks.reshape(1, N_KV, G, 1, 1))
    o = jnp.einsum('bkgtj,bjkd->btkgd', p.astype(vv.dtype), vv)
    return o.reshape(B, T, ATTN_Q)


def depthwise_causal(u_ext, w, b):
    out = lax.conv_general_dilated(u_ext, w[:, None, :].astype(u_ext.dtype), window_strides=(1,),
                                   padding='VALID', dimension_numbers=('NWC', 'WIO', 'NWC'),
                                   feature_group_count=CONV_CH)
    return out + b.astype(out.dtype)


def mem_cross_attn(q, mk, mv):
    B, T = q.shape[:2]
    s = jnp.einsum('bthd,bmhd->bhtm', q, mk).astype(jnp.float32) * (1.0 / math.sqrt(XHEAD_DIM))
    p = jax.nn.softmax(s, axis=-1)
    o = jnp.einsum('bhtm,bmhd->bthd', p.astype(mv.dtype), mv)
    return o.reshape(B, T, X_WIDTH)


def trunk(x, pos, mem_k, mem_v, conv_hist, swa_k_hist, swa_v_hist,
          norm_g, w_in_conv, conv_w, conv_b, ln_g, ln_b, w_in_attn, sinks, w_out, final_g):
    B, T = x.shape[:2]
    new_conv, new_k, new_v = [], [], []
    for i in range(DEPTH):
        j = i // 2
        h = rms_norm(x, norm_g[i])
        if i % 2 == 0:
            p = h @ w_in_conv[j]
            a, bg, z, xq, xz = jnp.split(p, [CONV_CH, 2 * CONV_CH, 3 * CONV_CH, 3 * CONV_CH + X_WIDTH], axis=-1)
            u = a * jax.nn.sigmoid(bg)
            hist = jnp.zeros((B, CONV_WIDTH - 1, CONV_CH), u.dtype) if conv_hist is None else conv_hist[j].astype(u.dtype)
            u_ext = jnp.concatenate([hist, u], axis=1)
            c = depthwise_causal(u_ext, conv_w[j], conv_b[j])
            mix = jax.nn.silu(layer_norm(c, ln_g[j], ln_b[j])) * jax.nn.silu(z)
            new_conv.append(u_ext[:, -(CONV_WIDTH - 1):])
        else:
            p = h @ w_in_attn[j]
            q, k, v, z, xq, xz = jnp.split(
                p, [ATTN_Q, ATTN_Q + ATTN_KV, ATTN_Q + 2 * ATTN_KV, 2 * ATTN_Q + 2 * ATTN_KV,
                    2 * ATTN_Q + 2 * ATTN_KV + X_WIDTH], axis=-1)
            q = rope(q.reshape(B, T, N_Q, HEAD_DIM), pos)
            k = rope(k.reshape(B, T, N_KV, HEAD_DIM), pos)
            v = v.reshape(B, T, N_KV, HEAD_DIM)
            if swa_k_hist is None:
                o = swa_banded(q, k, v, sinks[j])
                new_k.append(k[:, -WINDOW:])
                new_v.append(v[:, -WINDOW:])
            else:
                kk = jnp.concatenate([swa_k_hist[j].astype(k.dtype), k], axis=1)
                vv = jnp.concatenate([swa_v_hist[j].astype(v.dtype), v], axis=1)
                o = swa_with_past(q, kk, vv, sinks[j])
                new_k.append(kk[:, -WINDOW:])
                new_v.append(vv[:, -WINDOW:])
            mix = o * jax.nn.silu(z)
        xo = mem_cross_attn(xq.reshape(B, T, N_XH, XHEAD_DIM), mem_k[i].astype(xq.dtype),
                            mem_v[i].astype(xq.dtype)) * jax.nn.silu(xz)
        x = x + jnp.concatenate([mix, xo], axis=-1) @ w_out[i]
    return rms_norm(x, final_g), jnp.stack(new_conv), jnp.stack(new_k), jnp.stack(new_v)


def setup_inputs(seed: int = 0) -> dict:
    key = jax.random.key(seed)
    ks = jax.random.split(key, 20)
    f32 = jnp.float32

    def nrm(k, shape, s):
        return jax.random.normal(k, shape, f32) * s

    return {
        'x_prompt': nrm(ks[0], (BATCH, SEQ, D_MODEL), 1.0),
        'x_sample': nrm(ks[1], (DEC_BATCH, DEC_SEQ, D_MODEL), 1.0),
        'state_conv': nrm(ks[2], (N_CONV_LAYERS, DEC_BATCH, CONV_WIDTH - 1, CONV_CH), 0.5),
        'cache_swa_k': nrm(ks[3], (N_ATTN_LAYERS, DEC_BATCH, WINDOW, N_KV, HEAD_DIM), 1.0),
        'cache_swa_v': nrm(ks[4], (N_ATTN_LAYERS, DEC_BATCH, WINDOW, N_KV, HEAD_DIM), 1.0),
        'cache_mem_k': nrm(ks[5], (DEPTH, DEC_BATCH, N_MEM, N_XH, XHEAD_DIM), 1.0),
        'cache_mem_v': nrm(ks[6], (DEPTH, DEC_BATCH, N_MEM, N_XH, XHEAD_DIM), 1.0),
        'mem_prompt': nrm(ks[7], (BATCH, N_MEM, D_MODEL), 1.0),
        'norm_g': 1.0 + nrm(ks[8], (DEPTH, D_MODEL), 0.02),
        'w_in_conv': nrm(ks[9], (N_CONV_LAYERS, D_MODEL, CONV_IN), D_MODEL ** -0.5),
        'conv_w': nrm(ks[10], (N_CONV_LAYERS, CONV_WIDTH, CONV_CH), CONV_WIDTH ** -0.5),
        'conv_b': nrm(ks[11], (N_CONV_LAYERS, CONV_CH), 0.01),
        'ln_g': 1.0 + nrm(ks[12], (N_CONV_LAYERS, CONV_CH), 0.02),
        'ln_b': nrm(ks[13], (N_CONV_LAYERS, CONV_CH), 0.01),
        'w_in_attn': nrm(ks[14], (N_ATTN_LAYERS, D_MODEL, ATTN_IN), D_MODEL ** -0.5),
        'sinks': nrm(ks[15], (N_ATTN_LAYERS, N_Q), 1.0),
        'w_mem_kv': nrm(ks[16], (DEPTH, D_MODEL, 2 * X_WIDTH), D_MODEL ** -0.5),
        'w_out': nrm(ks[17], (DEPTH, BRANCH_W, D_MODEL), BRANCH_W ** -0.5),
        'final_g': 1.0 + nrm(ks[18], (D_MODEL,), 0.02),
    }


def reference(x_prompt, x_sample, state_conv, cache_swa_k, cache_swa_v, cache_mem_k, cache_mem_v,
              mem_prompt, norm_g, w_in_conv, conv_w, conv_b, ln_g, ln_b, w_in_attn, sinks,
              w_mem_kv, w_out, final_g):
    kv = jnp.einsum('bmd,lde->lbme', mem_prompt, w_mem_kv)
    mem_k_p = kv[..., :X_WIDTH].reshape(DEPTH, BATCH, N_MEM, N_XH, XHEAD_DIM)
    mem_v_p = kv[..., X_WIDTH:].reshape(DEPTH, BATCH, N_MEM, N_XH, XHEAD_DIM)

    pos_p = jnp.arange(x_prompt.shape[1], dtype=jnp.int32)
    y_prompt, conv_p, k_p, v_p = trunk(x_prompt, pos_p, mem_k_p, mem_v_p, None, None, None,
                                       norm_g, w_in_conv, conv_w, conv_b, ln_g, ln_b, w_in_attn,
                                       sinks, w_out, final_g)

    pos_s = PAST_LEN + jnp.arange(x_sample.shape[1], dtype=jnp.int32)
    y_sample, conv_s, k_s, v_s = trunk(x_sample, pos_s, cache_mem_k, cache_mem_v, state_conv,
                                       cache_swa_k, cache_swa_v,
                                       norm_g, w_in_conv, conv_w, conv_b, ln_g, ln_b, w_in_attn,
                                       sinks, w_out, final_g)
    return (y_prompt, y_sample, conv_p, conv_s, k_p, v_p, k_s, v_s, mem_k_p, mem_v_p)
```

```python
import functools
import math

import jax
import jax.numpy as jnp
from jax import lax
from jax.experimental import pallas as pl
from jax.experimental.pallas import tpu as pltpu

D_MODEL = 1024
DEPTH = 2
CHUNK = 64
N_MEM = 256
CONV_CH = D_MODEL
CONV_WIDTH = 31
HIST = CONV_WIDTH - 1
N_Q = 16
N_KV = 2
HEAD_DIM = 64
GROUP = N_Q // N_KV
WINDOW = 128
ROPE_THETA = 10000.0
N_XH = 4
XHEAD_DIM = 128
X_WIDTH = N_XH * XHEAD_DIM
ATTN_Q = N_Q * HEAD_DIM
ATTN_KV = N_KV * HEAD_DIM
CONV_IN = 3 * CONV_CH + 2 * X_WIDTH
ATTN_IN = 2 * ATTN_Q + 2 * ATTN_KV + 2 * X_WIDTH
BRANCH_W = CONV_CH + X_WIDTH
RMS_EPS = 1e-6
LN_EPS = 1e-5
PAST_LEN = 2048

LANES = 128
HIST_PAD = 32
KEY_SPAN = WINDOW + CHUNK
PAIRS = GROUP // 2
VMEM_LIMIT_BYTES = 56 * 1024 * 1024

F32 = jnp.float32
BF16 = jnp.bfloat16


def _bdot(a, b):
    return jnp.dot(a, b, preferred_element_type=F32)


def _bdot_nt(a, b):
    return lax.dot_general(a, b, (((1,), (1,)), ((), ())), preferred_element_type=F32)


def _rms(x, g):
    return x * lax.rsqrt(jnp.mean(x * x, axis=-1, keepdims=True) + RMS_EPS) * g


def _layer_norm(x, g, b):
    xc = x - jnp.mean(x, axis=-1, keepdims=True)
    return xc * lax.rsqrt(jnp.mean(xc * xc, axis=-1, keepdims=True) + LN_EPS) * g + b


def _silu(x):
    return x * jax.nn.sigmoid(x)


def _rope(x, cos, sin_signed):
    lane = lax.broadcasted_iota(jnp.int32, x.shape, 1)
    first_half = (lane % HEAD_DIM) < (HEAD_DIM // 2)
    rot = jnp.where(first_half, pltpu.roll(x, LANES - HEAD_DIM // 2, axis=1),
                    pltpu.roll(x, HEAD_DIM // 2, axis=1))
    return x * cos + rot * sin_signed


def _fill_kv(buf_a, buf_b, s, row0, val):
    rows = val.shape[0]
    lane = lax.broadcasted_iota(jnp.int32, val.shape, 1)
    lo = lane < HEAD_DIM
    swapped = pltpu.roll(val, HEAD_DIM, axis=1)
    zero = jnp.zeros_like(val)
    buf_a[0, s, row0:row0 + rows, :] = jnp.where(lo, val, zero).astype(BF16)
    buf_b[0, s, row0:row0 + rows, :] = jnp.where(lo, zero, swapped).astype(BF16)
    buf_a[1, s, row0:row0 + rows, :] = jnp.where(lo, swapped, zero).astype(BF16)
    buf_b[1, s, row0:row0 + rows, :] = jnp.where(lo, zero, val).astype(BF16)


def _mem_attn(xq, mk, mv):
    outs = []
    for h in range(N_XH):
        sl = slice(h * XHEAD_DIM, (h + 1) * XHEAD_DIM)
        s = _bdot_nt(xq[:, sl].astype(BF16), mk[:, sl]) * (1.0 / math.sqrt(XHEAD_DIM))
        p = jnp.exp(s - jnp.max(s, axis=-1, keepdims=True))
        den = jnp.sum(p, axis=-1, keepdims=True)
        outs.append(_bdot(p.astype(BF16), mv[:, sl]) * (1.0 / den))
    return jnp.concatenate(outs, axis=1)


def _swa_chunk(q_c, ka, kb, va, vb, s, w0, bias, sinks_ref):
    lane = lax.broadcasted_iota(jnp.int32, (CHUNK, LANES), 1)
    lo = lane < HEAD_DIM
    outs = []
    for h in range(N_KV):
        base = h * PAIRS
        qs = jnp.concatenate([q_c[:, (base + j) * LANES:(base + j + 1) * LANES] for j in range(PAIRS)],
                             axis=0).astype(BF16)
        s_even = _bdot_nt(qs, ka[h, s, w0:w0 + KEY_SPAN, :])
        s_odd = _bdot_nt(qs, kb[h, s, w0:w0 + KEY_SPAN, :])
        if bias is not None:
            s_even = s_even + bias
            s_odd = s_odd + bias
        probs = [[], []]
        inv = [[], []]
        for j in range(PAIRS):
            for par, s_all in enumerate((s_even, s_odd)):
                sj = s_all[j * CHUNK:(j + 1) * CHUNK]
                sink = sinks_ref[h * GROUP + 2 * j + par]
                m = jnp.maximum(jnp.max(sj, axis=-1, keepdims=True), sink)
                p = jnp.exp(sj - m)
                den = jnp.sum(p, axis=-1, keepdims=True) + jnp.exp(sink - m)
                probs[par].append(p.astype(BF16))
                inv[par].append(1.0 / den)
        o_pair = (_bdot(jnp.concatenate(probs[0], axis=0), va[h, s, w0:w0 + KEY_SPAN, :])
                  + _bdot(jnp.concatenate(probs[1], axis=0), vb[h, s, w0:w0 + KEY_SPAN, :]))
        for j in range(PAIRS):
            scale = jnp.where(lo, inv[0][j], inv[1][j])
            outs.append(o_pair[j * CHUNK:(j + 1) * CHUNK] * scale)
    return jnp.concatenate(outs, axis=1)


def _trunk_kernel(sinks_ref, x_ref, cos_ref, sin_ref, hist_ref, ck_ref, cv_ref, mkv_ref,
                  ng_ref, wc_ref, cw_ref, cb_ref, lg_ref, lb_ref, wa_ref, wo_ref, fg_ref,
                  y_ref, convo_ref, ko_ref, vo_ref,
                  uext, ka, kb, va, vb, *, nb, t, carry):
    m = nb * t
    n_chunk = t // CHUNK
    tile = pl.program_id(1) if carry else None

    def load_history():
        uext[:, HIST_PAD - HIST:HIST_PAD, :] = hist_ref[0]
        for s in range(nb):
            _fill_kv(ka, kb, s, 0, ck_ref[0, s])
            _fill_kv(va, vb, s, 0, cv_ref[0, s])

    if carry:
        pl.when(tile == 0)(load_history)
    else:
        load_history()

    x = x_ref[...].reshape(m, D_MODEL)

    h = _rms(x, ng_ref[0:1, :]).astype(BF16)
    u = _bdot(h, wc_ref[0, :, 0:CONV_CH]) * jax.nn.sigmoid(_bdot(h, wc_ref[0, :, CONV_CH:2 * CONV_CH]))
    uext[:, HIST_PAD:HIST_PAD + t, :] = u.reshape(nb, t, CONV_CH)
    convo_ref[0] = uext[:, HIST_PAD + t - HIST:HIST_PAD + t, :]
    acc = jnp.broadcast_to(cb_ref[...].reshape(1, 1, CONV_CH), (nb, t, CONV_CH))
    for k in range(CONV_WIDTH):
        r0 = HIST_PAD - HIST + k
        acc = acc + uext[:, r0:r0 + t, :] * cw_ref[0, k:k + 1, :].reshape(1, 1, CONV_CH)
    c = acc.reshape(m, CONV_CH)
    z = _bdot(h, wc_ref[0, :, 2 * CONV_CH:3 * CONV_CH])
    mix = _silu(_layer_norm(c, lg_ref[...], lb_ref[...])) * _silu(z)
    xq = _bdot(h, wc_ref[0, :, 3 * CONV_CH:3 * CONV_CH + X_WIDTH])
    xz = _bdot(h, wc_ref[0, :, 3 * CONV_CH + X_WIDTH:CONV_IN])
    xo = jnp.concatenate(
        [_mem_attn(xq[s * t:(s + 1) * t], mkv_ref[0, s, :, 0:X_WIDTH], mkv_ref[0, s, :, X_WIDTH:2 * X_WIDTH])
         for s in range(nb)], axis=0) * _silu(xz)
    x = x + _bdot(jnp.concatenate([mix.astype(BF16), xo.astype(BF16)], axis=1), wo_ref[0])

    h = _rms(x, ng_ref[1:2, :]).astype(BF16)
    cos = jnp.concatenate([cos_ref[...]] * nb, axis=0)
    sin = jnp.concatenate([sin_ref[...]] * nb, axis=0)
    q = _bdot(h, wa_ref[0, :, 0:ATTN_Q])
    q = jnp.concatenate(
        [_rope(q[:, g * LANES:(g + 1) * LANES], cos, sin) for g in range(ATTN_Q // LANES)], axis=1
    ) * (1.0 / math.sqrt(HEAD_DIM))
    k = _rope(_bdot(h, wa_ref[0, :, ATTN_Q:ATTN_Q + ATTN_KV]), cos, sin)
    v = _bdot(h, wa_ref[0, :, ATTN_Q + ATTN_KV:ATTN_Q + 2 * ATTN_KV])
    for s in range(nb):
        k_s = k[s * t:(s + 1) * t]
        v_s = v[s * t:(s + 1) * t]
        _fill_kv(ka, kb, s, WINDOW, k_s)
        _fill_kv(va, vb, s, WINDOW, v_s)
        if t >= WINDOW:
            ko_ref[0, s] = k_s[t - WINDOW:]
            vo_ref[0, s] = v_s[t - WINDOW:]
        else:
            ko_ref[0, s] = jnp.concatenate([ck_ref[0, s, t:WINDOW, :], k_s], axis=0)
            vo_ref[0, s] = jnp.concatenate([cv_ref[0, s, t:WINDOW, :], v_s], axis=0)

    col = lax.broadcasted_iota(jnp.int32, (1, KEY_SPAN), 1)
    o_rows = []
    for s in range(nb):
        for i in range(n_chunk):
            bias = None
            if carry and i * CHUNK < WINDOW:
                first_valid = jnp.where(tile == 0, WINDOW - i * CHUNK, 0)
                bias = jnp.where(col >= first_valid, 0.0, -jnp.inf).astype(F32)
            r0 = s * t + i * CHUNK
            o_rows.append(_swa_chunk(q[r0:r0 + CHUNK], ka, kb, va, vb, s, i * CHUNK, bias, sinks_ref))
    o = jnp.concatenate(o_rows, axis=0)
    z = _bdot(h, wa_ref[0, :, ATTN_Q + 2 * ATTN_KV:2 * ATTN_Q + 2 * ATTN_KV])
    mix = o * _silu(z)
    xq = _bdot(h, wa_ref[0, :, 2 * ATTN_Q + 2 * ATTN_KV:2 * ATTN_Q + 2 * ATTN_KV + X_WIDTH])
    xz = _bdot(h, wa_ref[0, :, 2 * ATTN_Q + 2 * ATTN_KV + X_WIDTH:ATTN_IN])
    xo = jnp.concatenate(
        [_mem_attn(xq[s * t:(s + 1) * t], mkv_ref[1, s, :, 0:X_WIDTH], mkv_ref[1, s, :, X_WIDTH:2 * X_WIDTH])
         for s in range(nb)], axis=0) * _silu(xz)
    x = x + _bdot(jnp.concatenate([mix.astype(BF16), xo.astype(BF16)], axis=1), wo_ref[1])

    y_ref[...] = _rms(x, fg_ref[...]).reshape(nb, t, D_MODEL)

    if carry:
        uext[:, 0:HIST_PAD, :] = uext[:, t:t + HIST_PAD, :]
        for buf in (ka, kb, va, vb):
            buf[:, :, 0:WINDOW, :] = buf[:, :, t:t + WINDOW, :]


def _trunk_call(x, cos, sin, hist, ck, cv, mkv, sinks, ng, wc, cw, cb, lg, lb, wa, wo, fg, *, nb, t, carry):
    n_streams, seq, _ = x.shape
    if carry:
        assert nb == 1 and seq % t == 0 and t >= WINDOW and t % CHUNK == 0
        grid = (n_streams, seq // t)
        per_stream = lambda b, i: b
        per_tile = lambda b, i: i
    else:
        assert seq == t and n_streams % nb == 0 and t % CHUNK == 0
        grid = (n_streams // nb,)
        per_stream = lambda g: g
        per_tile = lambda g: 0

    def stream_tile(*idx):
        return (per_stream(*idx), per_tile(*idx), 0)

    def stream4(*idx):
        return (0, per_stream(*idx), 0, 0)

    def tile2(*idx):
        return (per_tile(*idx), 0)

    def const(rank):
        return lambda *idx: (0,) * rank

    def resident(arr):
        return pl.BlockSpec(arr.shape, const(arr.ndim), pipeline_mode=pl.Buffered(1))

    def small(arr):
        return pl.BlockSpec(arr.shape, const(arr.ndim))

    in_specs = [
        pl.BlockSpec(memory_space=pltpu.SMEM),
        pl.BlockSpec((nb, t, D_MODEL), stream_tile),
        pl.BlockSpec((t, LANES), tile2),
        pl.BlockSpec((t, LANES), tile2),
        pl.BlockSpec((1, nb, HIST, CONV_CH), stream4),
        pl.BlockSpec((1, nb, WINDOW, ATTN_KV), stream4),
        pl.BlockSpec((1, nb, WINDOW, ATTN_KV), stream4),
        pl.BlockSpec((DEPTH, nb, N_MEM, 2 * X_WIDTH), stream4),
        small(ng), resident(wc), small(cw), small(cb), small(lg), small(lb),
        resident(wa), resident(wo), small(fg),
    ]
    out_shape = (
        jax.ShapeDtypeStruct((n_streams, seq, D_MODEL), F32),
        jax.ShapeDtypeStruct((1, n_streams, HIST, CONV_CH), F32),
        jax.ShapeDtypeStruct((1, n_streams, WINDOW, ATTN_KV), F32),
        jax.ShapeDtypeStruct((1, n_streams, WINDOW, ATTN_KV), F32),
    )
    out_specs = (
        pl.BlockSpec((nb, t, D_MODEL), stream_tile),
        pl.BlockSpec((1, nb, HIST, CONV_CH), stream4),
        pl.BlockSpec((1, nb, WINDOW, ATTN_KV), stream4),
        pl.BlockSpec((1, nb, WINDOW, ATTN_KV), stream4),
    )
    kv_buf = pltpu.VMEM((N_KV, nb, WINDOW + t, ATTN_KV), BF16)
    scratch = [pltpu.VMEM((nb, HIST_PAD + t, CONV_CH), F32), kv_buf, kv_buf, kv_buf, kv_buf]
    return pl.pallas_call(
        functools.partial(_trunk_kernel, nb=nb, t=t, carry=carry),
        grid=grid, in_specs=in_specs, out_specs=out_specs, out_shape=out_shape,
        scratch_shapes=scratch,
        compiler_params=pltpu.CompilerParams(
            dimension_semantics=("arbitrary",) * len(grid), vmem_limit_bytes=VMEM_LIMIT_BYTES),
        name="trunk_prompt" if carry else "trunk_sample",
    )(sinks, x, cos, sin, hist, ck, cv, mkv, ng, wc, cw, cb, lg, lb, wa, wo, fg)


def _mem_kv_kernel(mem_ref, w_ref, kv_ref, kv16_ref):
    kv = _bdot(mem_ref[...], w_ref[0])
    kv_ref[0] = kv
    kv16_ref[0] = kv.astype(BF16)


def _mem_kv(mem, w):
    rows = mem.shape[0]
    shape = (DEPTH, rows, 2 * X_WIDTH)
    return pl.pallas_call(
        _mem_kv_kernel,
        grid=(DEPTH,),
        in_specs=[pl.BlockSpec((rows, D_MODEL), lambda l: (0, 0)),
                  pl.BlockSpec((1, D_MODEL, 2 * X_WIDTH), lambda l: (l, 0, 0))],
        out_specs=(pl.BlockSpec((1, rows, 2 * X_WIDTH), lambda l: (l, 0, 0)),
                   pl.BlockSpec((1, rows, 2 * X_WIDTH), lambda l: (l, 0, 0))),
        out_shape=(jax.ShapeDtypeStruct(shape, F32), jax.ShapeDtypeStruct(shape, BF16)),
        compiler_params=pltpu.CompilerParams(dimension_semantics=("arbitrary",)),
        name="mem_kv",
    )(mem, w)


def _rope_tables(pos):
    half = HEAD_DIM // 2
    inv = ROPE_THETA ** (-jnp.arange(half, dtype=F32) / half)
    ang = pos.astype(F32)[:, None] * inv[None, :]
    cos, sin = jnp.cos(ang), jnp.sin(ang)
    reps = LANES // HEAD_DIM
    return jnp.tile(cos, (1, 2 * reps)), jnp.tile(jnp.concatenate([-sin, sin], axis=1), (1, reps))


PROMPT_TILE = 256
SAMPLE_STREAMS = 4


def kernel(x_prompt, x_sample, state_conv, cache_swa_k, cache_swa_v, cache_mem_k, cache_mem_v, mem_prompt, norm_g, w_in_conv, conv_w, conv_b, ln_g, ln_b, w_in_attn, sinks, w_mem_kv, w_out, final_g):
    batch, seq, _ = x_prompt.shape
    dec_batch, dec_seq, _ = x_sample.shape
    wc = w_in_conv.astype(BF16)
    wa = w_in_attn.astype(BF16)
    wo = w_out.astype(BF16)
    fg = final_g.reshape(1, D_MODEL)
    sinks1 = sinks.reshape(N_Q)
    weights = (sinks1, norm_g, wc, conv_w, conv_b, ln_g, ln_b, wa, wo, fg)

    kv, kv16 = _mem_kv(mem_prompt.reshape(batch * N_MEM, D_MODEL).astype(BF16), w_mem_kv.astype(BF16))
    mem_k_p = kv[..., :X_WIDTH].reshape(DEPTH, batch, N_MEM, N_XH, XHEAD_DIM)
    mem_v_p = kv[..., X_WIDTH:].reshape(DEPTH, batch, N_MEM, N_XH, XHEAD_DIM)

    cos_p, sin_p = _rope_tables(jnp.arange(seq, dtype=jnp.int32))
    y_p, conv_p, k_p, v_p = _trunk_call(
        x_prompt, cos_p, sin_p,
        jnp.zeros((1, batch, HIST, CONV_CH), F32),
        jnp.zeros((1, batch, WINDOW, ATTN_KV), F32), jnp.zeros((1, batch, WINDOW, ATTN_KV), F32),
        kv16.reshape(DEPTH, batch, N_MEM, 2 * X_WIDTH), *weights,
        nb=1, t=min(PROMPT_TILE, seq), carry=True)

    cos_s, sin_s = _rope_tables(PAST_LEN + jnp.arange(dec_seq, dtype=jnp.int32))
    mkv_s = jnp.concatenate([cache_mem_k.reshape(DEPTH, dec_batch, N_MEM, X_WIDTH).astype(BF16),
                             cache_mem_v.reshape(DEPTH, dec_batch, N_MEM, X_WIDTH).astype(BF16)], axis=-1)
    y_s, conv_s, k_s, v_s = _trunk_call(
        x_sample, cos_s, sin_s, state_conv,
        cache_swa_k.reshape(1, dec_batch, WINDOW, ATTN_KV), cache_swa_v.reshape(1, dec_batch, WINDOW, ATTN_KV),
        mkv_s, *weights, nb=SAMPLE_STREAMS, t=dec_seq, carry=False)

    kv_shape = lambda n: (1, n, WINDOW, N_KV, HEAD_DIM)
    return (y_p, y_s, conv_p, conv_s,
            k_p.reshape(kv_shape(batch)), v_p.reshape(kv_shape(batch)),
            k_s.reshape(kv_shape(dec_batch)), v_s.reshape(kv_shape(dec_batch)),
            mem_k_p, mem_v_p)
```

```python
import functools
import math

import jax
import jax.numpy as jnp
from jax import lax
from jax.experimental import pallas as pl
from jax.experimental.pallas import tpu as pltpu

D_MODEL = 1024
DEPTH = 2
CHUNK = 64
N_MEM = 256
CONV_CH = D_MODEL
CONV_WIDTH = 31
HIST = CONV_WIDTH - 1
N_Q = 16
N_KV = 2
HEAD_DIM = 64
GROUP = N_Q // N_KV
WINDOW = 128
ROPE_THETA = 10000.0
N_XH = 4
XHEAD_DIM = 128
X_WIDTH = N_XH * XHEAD_DIM
ATTN_Q = N_Q * HEAD_DIM
ATTN_KV = N_KV * HEAD_DIM
CONV_IN = 3 * CONV_CH + 2 * X_WIDTH
ATTN_IN = 2 * ATTN_Q + 2 * ATTN_KV + 2 * X_WIDTH
BRANCH_W = CONV_CH + X_WIDTH
RMS_EPS = 1e-6
LN_EPS = 1e-5
PAST_LEN = 2048

LANES = 128
SUBLANES = 8
HIST_PAD = 32
KEY_SPAN = WINDOW + CHUNK
PAIRS = GROUP // 2
VMEM_LIMIT_BYTES = 56 * 1024 * 1024

F32 = jnp.float32
BF16 = jnp.bfloat16


def _bdot(a, b):
    return jnp.dot(a, b, preferred_element_type=F32)


def _bdot_nt(a, b):
    return lax.dot_general(a, b, (((1,), (1,)), ((), ())), preferred_element_type=F32)


def _rms(x, g):
    return x * lax.rsqrt(jnp.mean(x * x, axis=-1, keepdims=True) + RMS_EPS) * g


def _layer_norm(x, g, b):
    xc = x - jnp.mean(x, axis=-1, keepdims=True)
    return xc * lax.rsqrt(jnp.mean(xc * xc, axis=-1, keepdims=True) + LN_EPS) * g + b


def _silu(x):
    return x * jax.nn.sigmoid(x)


def _rope(x, cos, sin_signed):
    lane = lax.broadcasted_iota(jnp.int32, x.shape, 1)
    first_half = (lane % HEAD_DIM) < (HEAD_DIM // 2)
    rot = jnp.where(first_half, pltpu.roll(x, LANES - HEAD_DIM // 2, axis=1),
                    pltpu.roll(x, HEAD_DIM // 2, axis=1))
    return x * cos + rot * sin_signed


def _fill_kv(buf_a, buf_b, s, row0, val):
    rows = val.shape[0]
    lane = lax.broadcasted_iota(jnp.int32, val.shape, 1)
    lo = lane < HEAD_DIM
    swapped = pltpu.roll(val, HEAD_DIM, axis=1)
    zero = jnp.zeros_like(val)
    buf_a[0, s, row0:row0 + rows, :] = jnp.where(lo, val, zero).astype(BF16)
    buf_b[0, s, row0:row0 + rows, :] = jnp.where(lo, zero, swapped).astype(BF16)
    buf_a[1, s, row0:row0 + rows, :] = jnp.where(lo, swapped, zero).astype(BF16)
    buf_b[1, s, row0:row0 + rows, :] = jnp.where(lo, zero, val).astype(BF16)


def _mem_attn(xq, mk, mv):
    outs = []
    for h in range(N_XH):
        sl = slice(h * XHEAD_DIM, (h + 1) * XHEAD_DIM)
        s = _bdot_nt(xq[:, sl].astype(BF16), mk[:, sl]) * (1.0 / math.sqrt(XHEAD_DIM))
        p = jnp.exp(s - jnp.max(s, axis=-1, keepdims=True))
        den = jnp.sum(p, axis=-1, keepdims=True)
        outs.append(_bdot(p.astype(BF16), mv[:, sl]) * (1.0 / den))
    return jnp.concatenate(outs, axis=1)


def _swa_chunk(q_c, ka, kb, va, vb, s, w0, bias, sinks_ref):
    lane = lax.broadcasted_iota(jnp.int32, (CHUNK, LANES), 1)
    lo = lane < HEAD_DIM
    outs = []
    for h in range(N_KV):
        base = h * PAIRS
        qs = jnp.concatenate([q_c[:, (base + j) * LANES:(base + j + 1) * LANES] for j in range(PAIRS)],
                             axis=0).astype(BF16)
        s_even = _bdot_nt(qs, ka[h, s, w0:w0 + KEY_SPAN, :])
        s_odd = _bdot_nt(qs, kb[h, s, w0:w0 + KEY_SPAN, :])
        if bias is not None:
            s_even = s_even + bias
            s_odd = s_odd + bias
        probs = [[], []]
        inv = [[], []]
        for j in range(PAIRS):
            for par, s_all in enumerate((s_even, s_odd)):
                sj = s_all[j * CHUNK:(j + 1) * CHUNK]
                sink = sinks_ref[h * GROUP + 2 * j + par]
                m = jnp.maximum(jnp.max(sj, axis=-1, keepdims=True), sink)
                p = jnp.exp(sj - m)
                den = jnp.sum(p, axis=-1, keepdims=True) + jnp.exp(sink - m)
                probs[par].append(p.astype(BF16))
                inv[par].append(1.0 / den)
        o_pair = (_bdot(jnp.concatenate(probs[0], axis=0), va[h, s, w0:w0 + KEY_SPAN, :])
                  + _bdot(jnp.concatenate(probs[1], axis=0), vb[h, s, w0:w0 + KEY_SPAN, :]))
        for j in range(PAIRS):
            scale = jnp.where(lo, inv[0][j], inv[1][j])
            outs.append(o_pair[j * CHUNK:(j + 1) * CHUNK] * scale)
    return jnp.concatenate(outs, axis=1)


def _trunk_kernel(sinks_ref, x_ref, cos_ref, sin_ref, hist_ref, ck_ref, cv_ref, mkv_ref,
                  ng_ref, wc_ref, cw_ref, cb_ref, lg_ref, lb_ref, wa_ref, wo_ref, fg_ref,
                  y_ref, convo_ref, ko_ref, vo_ref,
                  uext, ka, kb, va, vb, *, nb, t, carry):
    m = nb * t
    n_chunk = t // CHUNK
    tile = pl.program_id(1) if carry else None

    def load_history():
        uext[:, HIST_PAD - HIST:HIST_PAD, :] = hist_ref[0]
        for s in range(nb):
            _fill_kv(ka, kb, s, 0, ck_ref[0, s])
            _fill_kv(va, vb, s, 0, cv_ref[0, s])

    if carry:
        pl.when(tile == 0)(load_history)
    else:
        load_history()

    x = x_ref[...].reshape(m, D_MODEL)

    h = _rms(x, ng_ref[0:1, :]).astype(BF16)
    u = _bdot(h, wc_ref[0, :, 0:CONV_CH]) * jax.nn.sigmoid(_bdot(h, wc_ref[0, :, CONV_CH:2 * CONV_CH]))
    uext[:, HIST_PAD:HIST_PAD + t, :] = u.reshape(nb, t, CONV_CH)
    convo_ref[0] = uext[:, HIST_PAD + t - HIST:HIST_PAD + t, :]
    acc = jnp.broadcast_to(cb_ref[...].reshape(1, 1, CONV_CH), (nb, t, CONV_CH))
    for res in range(SUBLANES):
        taps = [k for k in range(CONV_WIDTH) if (HIST_PAD - HIST + k) % SUBLANES == res]
        lo_row = HIST_PAD - HIST + taps[0]
        hi_row = HIST_PAD - HIST + taps[-1] + t
        shifted = uext[:, lo_row:hi_row, :]
        part = None
        for k in taps:
            off = HIST_PAD - HIST + k - lo_row
            term = shifted[:, off:off + t, :] * cw_ref[0, k:k + 1, :].reshape(1, 1, CONV_CH)
            part = term if part is None else part + term
        acc = acc + part
    c = acc.reshape(m, CONV_CH)
    z = _bdot(h, wc_ref[0, :, 2 * CONV_CH:3 * CONV_CH])
    mix = _silu(_layer_norm(c, lg_ref[...], lb_ref[...])) * _silu(z)
    xq = _bdot(h, wc_ref[0, :, 3 * CONV_CH:3 * CONV_CH + X_WIDTH])
    xz = _bdot(h, wc_ref[0, :, 3 * CONV_CH + X_WIDTH:CONV_IN])
    xo = jnp.concatenate(
        [_mem_attn(xq[s * t:(s + 1) * t], mkv_ref[0, s, :, 0:X_WIDTH], mkv_ref[0, s, :, X_WIDTH:2 * X_WIDTH])
         for s in range(nb)], axis=0) * _silu(xz)
    x = x + _bdot(jnp.concatenate([mix.astype(BF16), xo.astype(BF16)], axis=1), wo_ref[0])

    h = _rms(x, ng_ref[1:2, :]).astype(BF16)
    cos = jnp.concatenate([cos_ref[...]] * nb, axis=0)
    sin = jnp.concatenate([sin_ref[...]] * nb, axis=0)
    q = _bdot(h, wa_ref[0, :, 0:ATTN_Q])
    q = jnp.concatenate(
        [_rope(q[:, g * LANES:(g + 1) * LANES], cos, sin) for g in range(ATTN_Q // LANES)], axis=1
    ) * (1.0 / math.sqrt(HEAD_DIM))
    k = _rope(_bdot(h, wa_ref[0, :, ATTN_Q:ATTN_Q + ATTN_KV]), cos, sin)
    v = _bdot(h, wa_ref[0, :, ATTN_Q + ATTN_KV:ATTN_Q + 2 * ATTN_KV])
    for s in range(nb):
        k_s = k[s * t:(s + 1) * t]
        v_s = v[s * t:(s + 1) * t]
        _fill_kv(ka, kb, s, WINDOW, k_s)
        _fill_kv(va, vb, s, WINDOW, v_s)
        if t >= WINDOW:
            ko_ref[0, s] = k_s[t - WINDOW:]
            vo_ref[0, s] = v_s[t - WINDOW:]
        else:
            ko_ref[0, s] = jnp.concatenate([ck_ref[0, s, t:WINDOW, :], k_s], axis=0)
            vo_ref[0, s] = jnp.concatenate([cv_ref[0, s, t:WINDOW, :], v_s], axis=0)

    col = lax.broadcasted_iota(jnp.int32, (1, KEY_SPAN), 1)
    o_rows = []
    for s in range(nb):
        for i in range(n_chunk):
            bias = None
            if carry and i * CHUNK < WINDOW:
                first_valid = jnp.where(tile == 0, WINDOW - i * CHUNK, 0)
                bias = jnp.where(col >= first_valid, 0.0, -jnp.inf).astype(F32)
            r0 = s * t + i * CHUNK
            o_rows.append(_swa_chunk(q[r0:r0 + CHUNK], ka, kb, va, vb, s, i * CHUNK, bias, sinks_ref))
    o = jnp.concatenate(o_rows, axis=0)
    z = _bdot(h, wa_ref[0, :, ATTN_Q + 2 * ATTN_KV:2 * ATTN_Q + 2 * ATTN_KV])
    mix = o * _silu(z)
    xq = _bdot(h, wa_ref[0, :, 2 * ATTN_Q + 2 * ATTN_KV:2 * ATTN_Q + 2 * ATTN_KV + X_WIDTH])
    xz = _bdot(h, wa_ref[0, :, 2 * ATTN_Q + 2 * ATTN_KV + X_WIDTH:ATTN_IN])
    xo = jnp.concatenate(
        [_mem_attn(xq[s * t:(s + 1) * t], mkv_ref[1, s, :, 0:X_WIDTH], mkv_ref[1, s, :, X_WIDTH:2 * X_WIDTH])
         for s in range(nb)], axis=0) * _silu(xz)
    x = x + _bdot(jnp.concatenate([mix.astype(BF16), xo.astype(BF16)], axis=1), wo_ref[1])

    y_ref[...] = _rms(x, fg_ref[...]).reshape(nb, t, D_MODEL)

    if carry:
        uext[:, 0:HIST_PAD, :] = uext[:, t:t + HIST_PAD, :]
        for buf in (ka, kb, va, vb):
            buf[:, :, 0:WINDOW, :] = buf[:, :, t:t + WINDOW, :]


def _trunk_call(x, cos, sin, hist, ck, cv, mkv, sinks, ng, wc, cw, cb, lg, lb, wa, wo, fg, *, nb, t, carry):
    n_streams, seq, _ = x.shape
    if carry:
        assert nb == 1 and seq % t == 0 and t >= WINDOW and t % CHUNK == 0
        grid = (n_streams, seq // t)
        per_stream = lambda b, i: b
        per_tile = lambda b, i: i
    else:
        assert seq == t and n_streams % nb == 0 and t % CHUNK == 0
        grid = (n_streams // nb,)
        per_stream = lambda g: g
        per_tile = lambda g: 0

    def stream_tile(*idx):
        return (per_stream(*idx), per_tile(*idx), 0)

    def stream4(*idx):
        return (0, per_stream(*idx), 0, 0)

    def tile2(*idx):
        return (per_tile(*idx), 0)

    def const(rank):
        return lambda *idx: (0,) * rank

    def resident(arr):
        return pl.BlockSpec(arr.shape, const(arr.ndim), pipeline_mode=pl.Buffered(1))

    def small(arr):
        return pl.BlockSpec(arr.shape, const(arr.ndim))

    in_specs = [
        pl.BlockSpec(memory_space=pltpu.SMEM),
        pl.BlockSpec((nb, t, D_MODEL), stream_tile),
        pl.BlockSpec((t, LANES), tile2),
        pl.BlockSpec((t, LANES), tile2),
        pl.BlockSpec((1, nb, HIST, CONV_CH), stream4),
        pl.BlockSpec((1, nb, WINDOW, ATTN_KV), stream4),
        pl.BlockSpec((1, nb, WINDOW, ATTN_KV), stream4),
        pl.BlockSpec((DEPTH, nb, N_MEM, 2 * X_WIDTH), stream4),
        small(ng), resident(wc), small(cw), small(cb), small(lg), small(lb),
        resident(wa), resident(wo), small(fg),
    ]
    out_shape = (
        jax.ShapeDtypeStruct((n_streams, seq, D_MODEL), F32),
        jax.ShapeDtypeStruct((1, n_streams, HIST, CONV_CH), F32),
        jax.ShapeDtypeStruct((1, n_streams, WINDOW, ATTN_KV), F32),
        jax.ShapeDtypeStruct((1, n_streams, WINDOW, ATTN_KV), F32),
    )
    out_specs = (
        pl.BlockSpec((nb, t, D_MODEL), stream_tile),
        pl.BlockSpec((1, nb, HIST, CONV_CH), stream4),
        pl.BlockSpec((1, nb, WINDOW, ATTN_KV), stream4),
        pl.BlockSpec((1, nb, WINDOW, ATTN_KV), stream4),
    )
    kv_buf = pltpu.VMEM((N_KV, nb, WINDOW + t, ATTN_KV), BF16)
    scratch = [pltpu.VMEM((nb, HIST_PAD + t, CONV_CH), F32), kv_buf, kv_buf, kv_buf, kv_buf]
    return pl.pallas_call(
        functools.partial(_trunk_kernel, nb=nb, t=t, carry=carry),
        grid=grid, in_specs=in_specs, out_specs=out_specs, out_shape=out_shape,
        scratch_shapes=scratch,
        compiler_params=pltpu.CompilerParams(
            dimension_semantics=("arbitrary",) * len(grid), vmem_limit_bytes=VMEM_LIMIT_BYTES),
        name="trunk_prompt" if carry else "trunk_sample",
    )(sinks, x, cos, sin, hist, ck, cv, mkv, ng, wc, cw, cb, lg, lb, wa, wo, fg)


def _mem_kv_kernel(mem_ref, w_ref, kv_ref, kv16_ref):
    kv = _bdot(mem_ref[...], w_ref[0])
    kv_ref[0] = kv
    kv16_ref[0] = kv.astype(BF16)


def _mem_kv(mem, w):
    rows = mem.shape[0]
    shape = (DEPTH, rows, 2 * X_WIDTH)
    return pl.pallas_call(
        _mem_kv_kernel,
        grid=(DEPTH,),
        in_specs=[pl.BlockSpec((rows, D_MODEL), lambda l: (0, 0)),
                  pl.BlockSpec((1, D_MODEL, 2 * X_WIDTH), lambda l: (l, 0, 0))],
        out_specs=(pl.BlockSpec((1, rows, 2 * X_WIDTH), lambda l: (l, 0, 0)),
                   pl.BlockSpec((1, rows, 2 * X_WIDTH), lambda l: (l, 0, 0))),
        out_shape=(jax.ShapeDtypeStruct(shape, F32), jax.ShapeDtypeStruct(shape, BF16)),
        compiler_params=pltpu.CompilerParams(dimension_semantics=("arbitrary",)),
        name="mem_kv",
    )(mem, w)


def _rope_tables(pos):
    half = HEAD_DIM // 2
    inv = ROPE_THETA ** (-jnp.arange(half, dtype=F32) / half)
    ang = pos.astype(F32)[:, None] * inv[None, :]
    cos, sin = jnp.cos(ang), jnp.sin(ang)
    reps = LANES // HEAD_DIM
    return jnp.tile(cos, (1, 2 * reps)), jnp.tile(jnp.concatenate([-sin, sin], axis=1), (1, reps))


PROMPT_TILE = 256
SAMPLE_STREAMS = 4


def kernel(x_prompt, x_sample, state_conv, cache_swa_k, cache_swa_v, cache_mem_k, cache_mem_v, mem_prompt, norm_g, w_in_conv, conv_w, conv_b, ln_g, ln_b, w_in_attn, sinks, w_mem_kv, w_out, final_g):
    batch, seq, _ = x_prompt.shape
    dec_batch, dec_seq, _ = x_sample.shape
    wc = w_in_conv.astype(BF16)
    wa = w_in_attn.astype(BF16)
    wo = w_out.astype(BF16)
    fg = final_g.reshape(1, D_MODEL)
    sinks1 = sinks.reshape(N_Q)
    weights = (sinks1, norm_g, wc, conv_w, conv_b, ln_g, ln_b, wa, wo, fg)

    kv, kv16 = _mem_kv(mem_prompt.reshape(batch * N_MEM, D_MODEL).astype(BF16), w_mem_kv.astype(BF16))
    mem_k_p = kv[..., :X_WIDTH].reshape(DEPTH, batch, N_MEM, N_XH, XHEAD_DIM)
    mem_v_p = kv[..., X_WIDTH:].reshape(DEPTH, batch, N_MEM, N_XH, XHEAD_DIM)

    cos_p, sin_p = _rope_tables(jnp.arange(seq, dtype=jnp.int32))
    y_p, conv_p, k_p, v_p = _trunk_call(
        x_prompt, cos_p, sin_p,
        jnp.zeros((1, batch, HIST, CONV_CH), F32),
        jnp.zeros((1, batch, WINDOW, ATTN_KV), F32), jnp.zeros((1, batch, WINDOW, ATTN_KV), F32),
        kv16.reshape(DEPTH, batch, N_MEM, 2 * X_WIDTH), *weights,
        nb=1, t=min(PROMPT_TILE, seq), carry=True)

    cos_s, sin_s = _rope_tables(PAST_LEN + jnp.arange(dec_seq, dtype=jnp.int32))
    mkv_s = jnp.concatenate([cache_mem_k.reshape(DEPTH, dec_batch, N_MEM, X_WIDTH).astype(BF16),
                             cache_mem_v.reshape(DEPTH, dec_batch, N_MEM, X_WIDTH).astype(BF16)], axis=-1)
    y_s, conv_s, k_s, v_s = _trunk_call(
        x_sample, cos_s, sin_s, state_conv,
        cache_swa_k.reshape(1, dec_batch, WINDOW, ATTN_KV), cache_swa_v.reshape(1, dec_batch, WINDOW, ATTN_KV),
        mkv_s, *weights, nb=SAMPLE_STREAMS, t=dec_seq, carry=False)

    kv_shape = lambda n: (1, n, WINDOW, N_KV, HEAD_DIM)
    return (y_p, y_s, conv_p, conv_s,
            k_p.reshape(kv_shape(batch)), v_p.reshape(kv_shape(batch)),
            k_s.reshape(kv_shape(dec_batch)), v_s.reshape(kv_shape(dec_batch)),
            mem_k_p, mem_v_p)
```

```python
import functools
import math

import jax
import jax.numpy as jnp
from jax import lax
from jax.experimental import pallas as pl
from jax.experimental.pallas import tpu as pltpu

D_MODEL = 1024
DEPTH = 2
CHUNK = 64
N_MEM = 256
CONV_CH = D_MODEL
CONV_WIDTH = 31
HIST = CONV_WIDTH - 1
N_Q = 16
N_KV = 2
HEAD_DIM = 64
GROUP = N_Q // N_KV
WINDOW = 128
ROPE_THETA = 10000.0
N_XH = 4
XHEAD_DIM = 128
X_WIDTH = N_XH * XHEAD_DIM
ATTN_Q = N_Q * HEAD_DIM
ATTN_KV = N_KV * HEAD_DIM
CONV_IN = 3 * CONV_CH + 2 * X_WIDTH
ATTN_IN = 2 * ATTN_Q + 2 * ATTN_KV + 2 * X_WIDTH
BRANCH_W = CONV_CH + X_WIDTH
RMS_EPS = 1e-6
LN_EPS = 1e-5
PAST_LEN = 2048

LANES = 128
SUBLANES = 8
HIST_PAD = 32
KEY_SPAN = WINDOW + CHUNK
PAIRS = GROUP // 2
CONV_BLOCK = 256
VMEM_LIMIT_BYTES = 56 * 1024 * 1024

F32 = jnp.float32
BF16 = jnp.bfloat16


def _bdot(a, b):
    return jnp.dot(a, b, preferred_element_type=F32)


def _bdot_nt(a, b):
    return lax.dot_general(a, b, (((1,), (1,)), ((), ())), preferred_element_type=F32)


def _rms(x, g):
    return x * lax.rsqrt(jnp.mean(x * x, axis=-1, keepdims=True) + RMS_EPS) * g


def _layer_norm(x, g, b):
    xc = x - jnp.mean(x, axis=-1, keepdims=True)
    return xc * lax.rsqrt(jnp.mean(xc * xc, axis=-1, keepdims=True) + LN_EPS) * g + b


def _silu(x):
    return x * jax.nn.sigmoid(x)


def _rope(x, cos, sin_signed):
    lane = lax.broadcasted_iota(jnp.int32, x.shape, 1)
    first_half = (lane % HEAD_DIM) < (HEAD_DIM // 2)
    rot = jnp.where(first_half, pltpu.roll(x, LANES - HEAD_DIM // 2, axis=1),
                    pltpu.roll(x, HEAD_DIM // 2, axis=1))
    return x * cos + rot * sin_signed


def _fill_kv(buf_a, buf_b, s, row0, val):
    rows = val.shape[0]
    lane = lax.broadcasted_iota(jnp.int32, val.shape, 1)
    lo = lane < HEAD_DIM
    swapped = pltpu.roll(val, HEAD_DIM, axis=1)
    zero = jnp.zeros_like(val)
    buf_a[0, s, row0:row0 + rows, :] = jnp.where(lo, val, zero).astype(BF16)
    buf_b[0, s, row0:row0 + rows, :] = jnp.where(lo, zero, swapped).astype(BF16)
    buf_a[1, s, row0:row0 + rows, :] = jnp.where(lo, swapped, zero).astype(BF16)
    buf_b[1, s, row0:row0 + rows, :] = jnp.where(lo, zero, val).astype(BF16)


def _mem_attn(xq, mk, mv):
    outs = []
    for h in range(N_XH):
        sl = slice(h * XHEAD_DIM, (h + 1) * XHEAD_DIM)
        s = _bdot_nt(xq[:, sl].astype(BF16), mk[:, sl].astype(BF16)) * (1.0 / math.sqrt(XHEAD_DIM))
        p = jnp.exp(s - jnp.max(s, axis=-1, keepdims=True))
        den = jnp.sum(p, axis=-1, keepdims=True)
        outs.append(_bdot(p.astype(BF16), mv[:, sl].astype(BF16)) * (1.0 / den))
    return jnp.concatenate(outs, axis=1)


def _swa_chunk(q_c, ka, kb, va, vb, s, w0, bias, sinks_ref):
    lane = lax.broadcasted_iota(jnp.int32, (CHUNK, LANES), 1)
    lo = lane < HEAD_DIM
    outs = []
    for h in range(N_KV):
        base = h * PAIRS
        qs = jnp.concatenate([q_c[:, (base + j) * LANES:(base + j + 1) * LANES] for j in range(PAIRS)],
                             axis=0).astype(BF16)
        s_even = _bdot_nt(qs, ka[h, s, w0:w0 + KEY_SPAN, :])
        s_odd = _bdot_nt(qs, kb[h, s, w0:w0 + KEY_SPAN, :])
        if bias is not None:
            s_even = s_even + bias
            s_odd = s_odd + bias
        probs = [[], []]
        inv = [[], []]
        for j in range(PAIRS):
            for par, s_all in enumerate((s_even, s_odd)):
                sj = s_all[j * CHUNK:(j + 1) * CHUNK]
                sink = sinks_ref[h * GROUP + 2 * j + par]
                m = jnp.maximum(jnp.max(sj, axis=-1, keepdims=True), sink)
                p = jnp.exp(sj - m)
                den = jnp.sum(p, axis=-1, keepdims=True) + jnp.exp(sink - m)
                probs[par].append(p.astype(BF16))
                inv[par].append(1.0 / den)
        o_pair = (_bdot(jnp.concatenate(probs[0], axis=0), va[h, s, w0:w0 + KEY_SPAN, :])
                  + _bdot(jnp.concatenate(probs[1], axis=0), vb[h, s, w0:w0 + KEY_SPAN, :]))
        for j in range(PAIRS):
            scale = jnp.where(lo, inv[0][j], inv[1][j])
            outs.append(o_pair[j * CHUNK:(j + 1) * CHUNK] * scale)
    return jnp.concatenate(outs, axis=1)


def _trunk_kernel(sinks_ref, x_ref, cos_ref, sin_ref, hist_ref, ck_ref, cv_ref, mk_ref, mv_ref,
                  ng_ref, wc_ref, cw_ref, cb_ref, lg_ref, lb_ref, wa_ref, wo_ref, fg_ref,
                  y_ref, convo_ref, ko_ref, vo_ref,
                  uext, ka, kb, va, vb, *, nb, t, carry):
    m = nb * t
    n_chunk = t // CHUNK
    tile = pl.program_id(1) if carry else None

    def load_history():
        uext[:, HIST_PAD - HIST:HIST_PAD, :] = hist_ref[0]
        for s in range(nb):
            _fill_kv(ka, kb, s, 0, ck_ref[0, s])
            _fill_kv(va, vb, s, 0, cv_ref[0, s])

    if carry:
        pl.when(tile == 0)(load_history)
    else:
        load_history()

    x = x_ref[...].reshape(m, D_MODEL)

    h = _rms(x, ng_ref[0:1, :]).astype(BF16)
    conv_blocks = []
    for c0 in range(0, CONV_CH, CONV_BLOCK):
        cs = slice(c0, c0 + CONV_BLOCK)
        u = (_bdot(h, wc_ref[0, :, c0:c0 + CONV_BLOCK])
             * jax.nn.sigmoid(_bdot(h, wc_ref[0, :, CONV_CH + c0:CONV_CH + c0 + CONV_BLOCK])))
        uext[:, HIST_PAD:HIST_PAD + t, cs] = u.reshape(nb, t, CONV_BLOCK)
        acc = jnp.broadcast_to(cb_ref[:, cs].reshape(1, 1, CONV_BLOCK), (nb, t, CONV_BLOCK))
        for res in range(SUBLANES):
            taps = [k for k in range(CONV_WIDTH) if (HIST_PAD - HIST + k) % SUBLANES == res]
            lo_row = HIST_PAD - HIST + taps[0]
            hi_row = HIST_PAD - HIST + taps[-1] + t
            shifted = uext[:, lo_row:hi_row, cs]
            part = None
            for k in taps:
                off = HIST_PAD - HIST + k - lo_row
                term = shifted[:, off:off + t, :] * cw_ref[0, k:k + 1, cs].reshape(1, 1, CONV_BLOCK)
                part = term if part is None else part + term
            acc = acc + part
        conv_blocks.append(acc.reshape(m, CONV_BLOCK))
    convo_ref[0] = uext[:, HIST_PAD + t - HIST:HIST_PAD + t, :]
    c = jnp.concatenate(conv_blocks, axis=1)
    z = _bdot(h, wc_ref[0, :, 2 * CONV_CH:3 * CONV_CH])
    mix = _silu(_layer_norm(c, lg_ref[...], lb_ref[...])) * _silu(z)
    xq = _bdot(h, wc_ref[0, :, 3 * CONV_CH:3 * CONV_CH + X_WIDTH])
    xz = _bdot(h, wc_ref[0, :, 3 * CONV_CH + X_WIDTH:CONV_IN])
    xo = jnp.concatenate(
        [_mem_attn(xq[s * t:(s + 1) * t], mk_ref[0, s], mv_ref[0, s])
         for s in range(nb)], axis=0) * _silu(xz)
    x = x + _bdot(jnp.concatenate([mix.astype(BF16), xo.astype(BF16)], axis=1), wo_ref[0])

    h = _rms(x, ng_ref[1:2, :]).astype(BF16)
    cos = jnp.concatenate([cos_ref[...]] * nb, axis=0)
    sin = jnp.concatenate([sin_ref[...]] * nb, axis=0)
    q = _bdot(h, wa_ref[0, :, 0:ATTN_Q])
    q = jnp.concatenate(
        [_rope(q[:, g * LANES:(g + 1) * LANES], cos, sin) for g in range(ATTN_Q // LANES)], axis=1
    ) * (1.0 / math.sqrt(HEAD_DIM))
    k = _rope(_bdot(h, wa_ref[0, :, ATTN_Q:ATTN_Q + ATTN_KV]), cos, sin)
    v = _bdot(h, wa_ref[0, :, ATTN_Q + ATTN_KV:ATTN_Q + 2 * ATTN_KV])
    for s in range(nb):
        k_s = k[s * t:(s + 1) * t]
        v_s = v[s * t:(s + 1) * t]
        _fill_kv(ka, kb, s, WINDOW, k_s)
        _fill_kv(va, vb, s, WINDOW, v_s)
        if t >= WINDOW:
            ko_ref[0, s] = k_s[t - WINDOW:]
            vo_ref[0, s] = v_s[t - WINDOW:]
        else:
            ko_ref[0, s] = jnp.concatenate([ck_ref[0, s, t:WINDOW, :], k_s], axis=0)
            vo_ref[0, s] = jnp.concatenate([cv_ref[0, s, t:WINDOW, :], v_s], axis=0)

    col = lax.broadcasted_iota(jnp.int32, (1, KEY_SPAN), 1)
    o_rows = []
    for s in range(nb):
        for i in range(n_chunk):
            bias = None
            if carry and i * CHUNK < WINDOW:
                first_valid = jnp.where(tile == 0, WINDOW - i * CHUNK, 0)
                bias = jnp.where(col >= first_valid, 0.0, -jnp.inf).astype(F32)
            r0 = s * t + i * CHUNK
            o_rows.append(_swa_chunk(q[r0:r0 + CHUNK], ka, kb, va, vb, s, i * CHUNK, bias, sinks_ref))
    o = jnp.concatenate(o_rows, axis=0)
    z = _bdot(h, wa_ref[0, :, ATTN_Q + 2 * ATTN_KV:2 * ATTN_Q + 2 * ATTN_KV])
    mix = o * _silu(z)
    xq = _bdot(h, wa_ref[0, :, 2 * ATTN_Q + 2 * ATTN_KV:2 * ATTN_Q + 2 * ATTN_KV + X_WIDTH])
    xz = _bdot(h, wa_ref[0, :, 2 * ATTN_Q + 2 * ATTN_KV + X_WIDTH:ATTN_IN])
    xo = jnp.concatenate(
        [_mem_attn(xq[s * t:(s + 1) * t], mk_ref[1, s], mv_ref[1, s])
         for s in range(nb)], axis=0) * _silu(xz)
    x = x + _bdot(jnp.concatenate([mix.astype(BF16), xo.astype(BF16)], axis=1), wo_ref[1])

    y_ref[...] = _rms(x, fg_ref[...]).reshape(nb, t, D_MODEL)

    if carry:
        uext[:, 0:HIST_PAD, :] = uext[:, t:t + HIST_PAD, :]
        for buf in (ka, kb, va, vb):
            buf[:, :, 0:WINDOW, :] = buf[:, :, t:t + WINDOW, :]


def _trunk_call(x, cos, sin, hist, ck, cv, mk, mv, v_block, sinks, ng, wc, cw, cb, lg, lb, wa, wo, fg,
                *, nb, t, carry):
    n_streams, seq, _ = x.shape
    if carry:
        assert nb == 1 and seq % t == 0 and t >= WINDOW and t % CHUNK == 0
        grid = (n_streams, seq // t)
        per_stream = lambda b, i: b
        per_tile = lambda b, i: i
    else:
        assert seq == t and n_streams % nb == 0 and t % CHUNK == 0
        grid = (n_streams // nb,)
        per_stream = lambda g: g
        per_tile = lambda g: 0

    def stream_tile(*idx):
        return (per_stream(*idx), per_tile(*idx), 0)

    def stream4(*idx):
        return (0, per_stream(*idx), 0, 0)

    def tile2(*idx):
        return (per_tile(*idx), 0)

    def const(rank):
        return lambda *idx: (0,) * rank

    def resident(arr):
        return pl.BlockSpec(arr.shape, const(arr.ndim), pipeline_mode=pl.Buffered(1))

    def small(arr):
        return pl.BlockSpec(arr.shape, const(arr.ndim))

    in_specs = [
        pl.BlockSpec(memory_space=pltpu.SMEM),
        pl.BlockSpec((nb, t, D_MODEL), stream_tile),
        pl.BlockSpec((t, LANES), tile2),
        pl.BlockSpec((t, LANES), tile2),
        pl.BlockSpec((1, nb, HIST, CONV_CH), stream4),
        pl.BlockSpec((1, nb, WINDOW, ATTN_KV), stream4),
        pl.BlockSpec((1, nb, WINDOW, ATTN_KV), stream4),
        pl.BlockSpec((DEPTH, nb, N_MEM, X_WIDTH), stream4, pipeline_mode=pl.Buffered(1)),
        pl.BlockSpec((DEPTH, nb, N_MEM, X_WIDTH), lambda *idx: (0, per_stream(*idx), 0, v_block),
                     pipeline_mode=pl.Buffered(1)),
        small(ng), resident(wc), small(cw), small(cb), small(lg), small(lb),
        resident(wa), resident(wo), small(fg),
    ]
    out_shape = (
        jax.ShapeDtypeStruct((n_streams, seq, D_MODEL), F32),
        jax.ShapeDtypeStruct((1, n_streams, HIST, CONV_CH), F32),
        jax.ShapeDtypeStruct((1, n_streams, WINDOW, ATTN_KV), F32),
        jax.ShapeDtypeStruct((1, n_streams, WINDOW, ATTN_KV), F32),
    )
    out_specs = (
        pl.BlockSpec((nb, t, D_MODEL), stream_tile),
        pl.BlockSpec((1, nb, HIST, CONV_CH), stream4),
        pl.BlockSpec((1, nb, WINDOW, ATTN_KV), stream4),
        pl.BlockSpec((1, nb, WINDOW, ATTN_KV), stream4),
    )
    kv_buf = pltpu.VMEM((N_KV, nb, WINDOW + t, ATTN_KV), BF16)
    scratch = [pltpu.VMEM((nb, HIST_PAD + t, CONV_CH), F32), kv_buf, kv_buf, kv_buf, kv_buf]
    return pl.pallas_call(
        functools.partial(_trunk_kernel, nb=nb, t=t, carry=carry),
        grid=grid, in_specs=in_specs, out_specs=out_specs, out_shape=out_shape,
        scratch_shapes=scratch,
        compiler_params=pltpu.CompilerParams(
            dimension_semantics=("arbitrary",) * len(grid), vmem_limit_bytes=VMEM_LIMIT_BYTES),
        name="trunk_prompt" if carry else "trunk_sample",
    )(sinks, x, cos, sin, hist, ck, cv, mk, mv, ng, wc, cw, cb, lg, lb, wa, wo, fg)


def _mem_kv_kernel(mem_ref, w_ref, k_ref, v_ref, kv16_ref):
    kv = _bdot(mem_ref[...], w_ref[0])
    kv16_ref[0] = kv.astype(BF16)
    for h in range(N_XH):
        k_ref[0, :, h, :] = kv[:, h * XHEAD_DIM:(h + 1) * XHEAD_DIM]
        v_ref[0, :, h, :] = kv[:, X_WIDTH + h * XHEAD_DIM:X_WIDTH + (h + 1) * XHEAD_DIM]


def _mem_kv(mem, w):
    rows = mem.shape[0]
    head_shape = (DEPTH, rows, N_XH, XHEAD_DIM)
    head_spec = pl.BlockSpec((1, rows, N_XH, XHEAD_DIM), lambda l: (l, 0, 0, 0))
    return pl.pallas_call(
        _mem_kv_kernel,
        grid=(DEPTH,),
        in_specs=[pl.BlockSpec((rows, D_MODEL), lambda l: (0, 0)),
                  pl.BlockSpec((1, D_MODEL, 2 * X_WIDTH), lambda l: (l, 0, 0))],
        out_specs=(head_spec, head_spec, pl.BlockSpec((1, rows, 2 * X_WIDTH), lambda l: (l, 0, 0))),
        out_shape=(jax.ShapeDtypeStruct(head_shape, F32), jax.ShapeDtypeStruct(head_shape, F32),
                   jax.ShapeDtypeStruct((DEPTH, rows, 2 * X_WIDTH), BF16)),
        compiler_params=pltpu.CompilerParams(dimension_semantics=("arbitrary",)),
        name="mem_kv",
    )(mem, w)


def _rope_tables(pos):
    half = HEAD_DIM // 2
    inv = ROPE_THETA ** (-jnp.arange(half, dtype=F32) / half)
    ang = pos.astype(F32)[:, None] * inv[None, :]
    cos, sin = jnp.cos(ang), jnp.sin(ang)
    reps = LANES // HEAD_DIM
    return jnp.tile(cos, (1, 2 * reps)), jnp.tile(jnp.concatenate([-sin, sin], axis=1), (1, reps))


PROMPT_TILE = 256
SAMPLE_STREAMS = 4


def kernel(x_prompt, x_sample, state_conv, cache_swa_k, cache_swa_v, cache_mem_k, cache_mem_v, mem_prompt, norm_g, w_in_conv, conv_w, conv_b, ln_g, ln_b, w_in_attn, sinks, w_mem_kv, w_out, final_g):
    batch, seq, _ = x_prompt.shape
    dec_batch, dec_seq, _ = x_sample.shape
    wc = w_in_conv.astype(BF16)
    wa = w_in_attn.astype(BF16)
    wo = w_out.astype(BF16)
    fg = final_g.reshape(1, D_MODEL)
    sinks1 = sinks.reshape(N_Q)
    weights = (sinks1, norm_g, wc, conv_w, conv_b, ln_g, ln_b, wa, wo, fg)

    mem_k_p, mem_v_p, kv16 = _mem_kv(mem_prompt.reshape(batch * N_MEM, D_MODEL).astype(BF16),
                                     w_mem_kv.astype(BF16))
    kv16 = kv16.reshape(DEPTH, batch, N_MEM, 2 * X_WIDTH)
    mem_k_p = mem_k_p.reshape(DEPTH, batch, N_MEM, N_XH, XHEAD_DIM)
    mem_v_p = mem_v_p.reshape(DEPTH, batch, N_MEM, N_XH, XHEAD_DIM)

    cos_p, sin_p = _rope_tables(jnp.arange(seq, dtype=jnp.int32))
    y_p, conv_p, k_p, v_p = _trunk_call(
        x_prompt, cos_p, sin_p,
        jnp.zeros((1, batch, HIST, CONV_CH), F32),
        jnp.zeros((1, batch, WINDOW, ATTN_KV), F32), jnp.zeros((1, batch, WINDOW, ATTN_KV), F32),
        kv16, kv16, 1, *weights,
        nb=1, t=min(PROMPT_TILE, seq), carry=True)

    cos_s, sin_s = _rope_tables(PAST_LEN + jnp.arange(dec_seq, dtype=jnp.int32))
    y_s, conv_s, k_s, v_s = _trunk_call(
        x_sample, cos_s, sin_s, state_conv,
        cache_swa_k.reshape(1, dec_batch, WINDOW, ATTN_KV), cache_swa_v.reshape(1, dec_batch, WINDOW, ATTN_KV),
        cache_mem_k.reshape(DEPTH, dec_batch, N_MEM, X_WIDTH), cache_mem_v.reshape(DEPTH, dec_batch, N_MEM, X_WIDTH),
        0, *weights, nb=SAMPLE_STREAMS, t=dec_seq, carry=False)

    kv_shape = lambda n: (1, n, WINDOW, N_KV, HEAD_DIM)
    return (y_p, y_s, conv_p, conv_s,
            k_p.reshape(kv_shape(batch)), v_p.reshape(kv_shape(batch)),
            k_s.reshape(kv_shape(dec_batch)), v_s.reshape(kv_shape(dec_batch)),
            mem_k_p, mem_v_p)
```

```python
import functools
import math

import jax
import jax.numpy as jnp
from jax import lax
from jax.experimental import pallas as pl
from jax.experimental.pallas import tpu as pltpu

D_MODEL = 1024
DEPTH = 2
CHUNK = 64
N_MEM = 256
CONV_CH = D_MODEL
CONV_WIDTH = 31
HIST = CONV_WIDTH - 1
N_Q = 16
N_KV = 2
HEAD_DIM = 64
GROUP = N_Q // N_KV
WINDOW = 128
ROPE_THETA = 10000.0
N_XH = 4
XHEAD_DIM = 128
X_WIDTH = N_XH * XHEAD_DIM
ATTN_Q = N_Q * HEAD_DIM
ATTN_KV = N_KV * HEAD_DIM
CONV_IN = 3 * CONV_CH + 2 * X_WIDTH
ATTN_IN = 2 * ATTN_Q + 2 * ATTN_KV + 2 * X_WIDTH
BRANCH_W = CONV_CH + X_WIDTH
RMS_EPS = 1e-6
LN_EPS = 1e-5
PAST_LEN = 2048

LANES = 128
SUBLANES = 8
HIST_PAD = 32
KEY_SPAN = WINDOW + CHUNK
PAIRS = GROUP // 2
CONV_BLOCK = 256
VMEM_LIMIT_BYTES = 60 * 1024 * 1024
LOG2E = math.log2(math.e)

F32 = jnp.float32
BF16 = jnp.bfloat16


def _bdot(a, b):
    return jnp.dot(a, b, preferred_element_type=F32)


def _bdot_nt(a, b):
    return lax.dot_general(a, b, (((1,), (1,)), ((), ())), preferred_element_type=F32)


def _rms(x, g):
    return x * lax.rsqrt(jnp.mean(x * x, axis=-1, keepdims=True) + RMS_EPS) * g


def _layer_norm(x, g, b):
    xc = x - jnp.mean(x, axis=-1, keepdims=True)
    return xc * lax.rsqrt(jnp.mean(xc * xc, axis=-1, keepdims=True) + LN_EPS) * g + b


def _silu(x):
    return x * jax.nn.sigmoid(x)


def _rope(x, cos, sin_signed):
    lane = lax.broadcasted_iota(jnp.int32, x.shape, 1)
    first_half = (lane % HEAD_DIM) < (HEAD_DIM // 2)
    rot = jnp.where(first_half, pltpu.roll(x, LANES - HEAD_DIM // 2, axis=1),
                    pltpu.roll(x, HEAD_DIM // 2, axis=1))
    return x * cos + rot * sin_signed


def _fill_kv(buf_a, buf_b, s, row0, val):
    rows = val.shape[0]
    lane = lax.broadcasted_iota(jnp.int32, val.shape, 1)
    lo = lane < HEAD_DIM
    swapped = pltpu.roll(val, HEAD_DIM, axis=1)
    zero = jnp.zeros_like(val)
    buf_a[0, s, row0:row0 + rows, :] = jnp.where(lo, val, zero).astype(BF16)
    buf_b[0, s, row0:row0 + rows, :] = jnp.where(lo, zero, swapped).astype(BF16)
    buf_a[1, s, row0:row0 + rows, :] = jnp.where(lo, swapped, zero).astype(BF16)
    buf_b[1, s, row0:row0 + rows, :] = jnp.where(lo, zero, val).astype(BF16)


def _mem_slab(ref, layer, s):
    if len(ref.shape) == 5:
        return jnp.concatenate([ref[layer, s, :, h, :] for h in range(N_XH)], axis=1)
    return ref[layer, s]


def _mem_attn(xq, mk, mv):
    outs = []
    for h in range(N_XH):
        sl = slice(h * XHEAD_DIM, (h + 1) * XHEAD_DIM)
        s = _bdot_nt(xq[:, sl].astype(BF16), mk[:, sl].astype(BF16)) * (LOG2E / math.sqrt(XHEAD_DIM))
        p = jnp.exp2(s - jnp.max(s, axis=-1, keepdims=True))
        den = jnp.sum(p, axis=-1, keepdims=True)
        outs.append(_bdot(p.astype(BF16), mv[:, sl].astype(BF16)) * (1.0 / den))
    return jnp.concatenate(outs, axis=1)


def _swa_chunk(q_c, ka, kb, va, vb, s, w0, bias, sinks_ref):
    lane = lax.broadcasted_iota(jnp.int32, (CHUNK, LANES), 1)
    lo = lane < HEAD_DIM
    outs = []
    for h in range(N_KV):
        base = h * PAIRS
        qs = jnp.concatenate([q_c[:, (base + j) * LANES:(base + j + 1) * LANES] for j in range(PAIRS)],
                             axis=0).astype(BF16)
        s_even = _bdot_nt(qs, ka[h, s, w0:w0 + KEY_SPAN, :])
        s_odd = _bdot_nt(qs, kb[h, s, w0:w0 + KEY_SPAN, :])
        if bias is not None:
            s_even = s_even + bias
            s_odd = s_odd + bias
        probs = [[], []]
        inv = [[], []]
        for j in range(PAIRS):
            for par, s_all in enumerate((s_even, s_odd)):
                sj = s_all[j * CHUNK:(j + 1) * CHUNK]
                sink = sinks_ref[h * GROUP + 2 * j + par] * LOG2E
                m = jnp.maximum(jnp.max(sj, axis=-1, keepdims=True), sink)
                p = jnp.exp2(sj - m)
                den = jnp.sum(p, axis=-1, keepdims=True) + jnp.exp2(sink - m)
                probs[par].append(p.astype(BF16))
                inv[par].append(1.0 / den)
        o_pair = (_bdot(jnp.concatenate(probs[0], axis=0), va[h, s, w0:w0 + KEY_SPAN, :])
                  + _bdot(jnp.concatenate(probs[1], axis=0), vb[h, s, w0:w0 + KEY_SPAN, :]))
        for j in range(PAIRS):
            scale = jnp.where(lo, inv[0][j], inv[1][j])
            outs.append(o_pair[j * CHUNK:(j + 1) * CHUNK] * scale)
    return jnp.concatenate(outs, axis=1)


def _trunk_kernel(sinks_ref, x_ref, cos_ref, sin_ref, hist_ref, ck_ref, cv_ref, mk_ref, mv_ref,
                  ng_ref, wc_ref, cw_ref, cb_ref, lg_ref, lb_ref, wa_ref, wo_ref, fg_ref,
                  y_ref, convo_ref, ko_ref, vo_ref,
                  uext, ka, kb, va, vb, *, nb, t, carry):
    m = nb * t
    n_chunk = t // CHUNK
    tile = pl.program_id(1) if carry else None

    def load_history():
        uext[:, HIST_PAD - HIST:HIST_PAD, :] = hist_ref[0]
        for s in range(nb):
            _fill_kv(ka, kb, s, 0, ck_ref[0, s])
            _fill_kv(va, vb, s, 0, cv_ref[0, s])

    if carry:
        pl.when(tile == 0)(load_history)
    else:
        load_history()

    x = x_ref[...].reshape(m, D_MODEL)

    h = _rms(x, ng_ref[0:1, :]).astype(BF16)
    conv_blocks = []
    for c0 in range(0, CONV_CH, CONV_BLOCK):
        cs = slice(c0, c0 + CONV_BLOCK)
        u = (_bdot(h, wc_ref[0, :, c0:c0 + CONV_BLOCK])
             * jax.nn.sigmoid(_bdot(h, wc_ref[0, :, CONV_CH + c0:CONV_CH + c0 + CONV_BLOCK])))
        uext[:, HIST_PAD:HIST_PAD + t, cs] = u.reshape(nb, t, CONV_BLOCK)
        acc = jnp.broadcast_to(cb_ref[:, cs].reshape(1, 1, CONV_BLOCK), (nb, t, CONV_BLOCK))
        for res in range(SUBLANES):
            taps = [k for k in range(CONV_WIDTH) if (HIST_PAD - HIST + k) % SUBLANES == res]
            lo_row = HIST_PAD - HIST + taps[0]
            hi_row = HIST_PAD - HIST + taps[-1] + t
            shifted = uext[:, lo_row:hi_row, cs]
            part = None
            for k in taps:
                off = HIST_PAD - HIST + k - lo_row
                term = shifted[:, off:off + t, :] * cw_ref[0, k:k + 1, cs].reshape(1, 1, CONV_BLOCK)
                part = term if part is None else part + term
            acc = acc + part
        conv_blocks.append(acc.reshape(m, CONV_BLOCK))
    convo_ref[0] = uext[:, HIST_PAD + t - HIST:HIST_PAD + t, :]
    c = jnp.concatenate(conv_blocks, axis=1)
    z = _bdot(h, wc_ref[0, :, 2 * CONV_CH:3 * CONV_CH])
    mix = _silu(_layer_norm(c, lg_ref[...], lb_ref[...])) * _silu(z)
    xq = _bdot(h, wc_ref[0, :, 3 * CONV_CH:3 * CONV_CH + X_WIDTH])
    xz = _bdot(h, wc_ref[0, :, 3 * CONV_CH + X_WIDTH:CONV_IN])
    xo = jnp.concatenate(
        [_mem_attn(xq[s * t:(s + 1) * t], _mem_slab(mk_ref, 0, s), _mem_slab(mv_ref, 0, s))
         for s in range(nb)], axis=0) * _silu(xz)
    x = x + _bdot(jnp.concatenate([mix.astype(BF16), xo.astype(BF16)], axis=1), wo_ref[0, :, 0:D_MODEL])

    h = _rms(x, ng_ref[1:2, :]).astype(BF16)
    cos = jnp.concatenate([cos_ref[...]] * nb, axis=0)
    sin = jnp.concatenate([sin_ref[...]] * nb, axis=0)
    q = _bdot(h, wa_ref[0, :, 0:ATTN_Q])
    q = jnp.concatenate(
        [_rope(q[:, g * LANES:(g + 1) * LANES], cos, sin) for g in range(ATTN_Q // LANES)], axis=1
    ) * (LOG2E / math.sqrt(HEAD_DIM))
    k = _rope(_bdot(h, wa_ref[0, :, ATTN_Q:ATTN_Q + ATTN_KV]), cos, sin)
    v = _bdot(h, wa_ref[0, :, ATTN_Q + ATTN_KV:ATTN_Q + 2 * ATTN_KV])
    for s in range(nb):
        k_s = k[s * t:(s + 1) * t]
        v_s = v[s * t:(s + 1) * t]
        _fill_kv(ka, kb, s, WINDOW, k_s)
        _fill_kv(va, vb, s, WINDOW, v_s)
        if t >= WINDOW:
            ko_ref[0, s] = k_s[t - WINDOW:]
            vo_ref[0, s] = v_s[t - WINDOW:]
        else:
            ko_ref[0, s] = jnp.concatenate([ck_ref[0, s, t:WINDOW, :], k_s], axis=0)
            vo_ref[0, s] = jnp.concatenate([cv_ref[0, s, t:WINDOW, :], v_s], axis=0)

    col = lax.broadcasted_iota(jnp.int32, (1, KEY_SPAN), 1)
    o_rows = []
    for s in range(nb):
        for i in range(n_chunk):
            bias = None
            if carry and i * CHUNK < WINDOW:
                first_valid = jnp.where(tile == 0, WINDOW - i * CHUNK, 0)
                bias = jnp.where(col >= first_valid, 0.0, -jnp.inf).astype(F32)
            r0 = s * t + i * CHUNK
            o_rows.append(_swa_chunk(q[r0:r0 + CHUNK], ka, kb, va, vb, s, i * CHUNK, bias, sinks_ref))
    o = jnp.concatenate(o_rows, axis=0)
    z = _bdot(h, wa_ref[0, :, ATTN_Q + 2 * ATTN_KV:2 * ATTN_Q + 2 * ATTN_KV])
    mix = o * _silu(z)
    xq = _bdot(h, wa_ref[0, :, 2 * ATTN_Q + 2 * ATTN_KV:2 * ATTN_Q + 2 * ATTN_KV + X_WIDTH])
    xz = _bdot(h, wa_ref[0, :, 2 * ATTN_Q + 2 * ATTN_KV + X_WIDTH:ATTN_IN])
    xo = jnp.concatenate(
        [_mem_attn(xq[s * t:(s + 1) * t], _mem_slab(mk_ref, 1, s), _mem_slab(mv_ref, 1, s))
         for s in range(nb)], axis=0) * _silu(xz)
    x = x + _bdot(jnp.concatenate([mix.astype(BF16), xo.astype(BF16)], axis=1), wo_ref[1, :, 0:D_MODEL])

    y_ref[...] = _rms(x, fg_ref[...]).reshape(nb, t, D_MODEL)

    if carry:
        uext[:, 0:HIST_PAD, :] = uext[:, t:t + HIST_PAD, :]
        for buf in (ka, kb, va, vb):
            buf[:, :, 0:WINDOW, :] = buf[:, :, t:t + WINDOW, :]


def _trunk_call(x, cos, sin, hist, ck, cv, mk, mv, v_block, sinks, ng, wc, cw, cb, lg, lb, wa, wo, fg,
                *, nb, t, carry):
    n_streams, seq, _ = x.shape
    if carry:
        assert nb == 1 and seq % t == 0 and t >= WINDOW and t % CHUNK == 0
        grid = (n_streams, seq // t)
        per_stream = lambda b, i: b
        per_tile = lambda b, i: i
    else:
        assert seq == t and n_streams % nb == 0 and t % CHUNK == 0
        grid = (n_streams // nb,)
        per_stream = lambda g: g
        per_tile = lambda g: 0

    def stream_tile(*idx):
        return (per_stream(*idx), per_tile(*idx), 0)

    def stream4(*idx):
        return (0, per_stream(*idx), 0, 0)

    def tile2(*idx):
        return (per_tile(*idx), 0)

    def const(rank):
        return lambda *idx: (0,) * rank

    def resident(arr):
        return pl.BlockSpec(arr.shape, const(arr.ndim), pipeline_mode=pl.Buffered(1))

    def small(arr):
        return pl.BlockSpec(arr.shape, const(arr.ndim))

    def mem_spec(arr, last_block):
        if arr.ndim == 5:
            return pl.BlockSpec((DEPTH, nb, N_MEM, N_XH, XHEAD_DIM), lambda *idx: (0, per_stream(*idx), 0, 0, 0),
                                pipeline_mode=pl.Buffered(1))
        return pl.BlockSpec((DEPTH, nb, N_MEM, X_WIDTH), lambda *idx: (0, per_stream(*idx), 0, last_block),
                            pipeline_mode=pl.Buffered(1))

    in_specs = [
        pl.BlockSpec(memory_space=pltpu.SMEM),
        pl.BlockSpec((nb, t, D_MODEL), stream_tile),
        pl.BlockSpec((t, LANES), tile2),
        pl.BlockSpec((t, LANES), tile2),
        pl.BlockSpec((1, nb, HIST, CONV_CH), stream4),
        pl.BlockSpec((1, nb, WINDOW, ATTN_KV), stream4),
        pl.BlockSpec((1, nb, WINDOW, ATTN_KV), stream4),
        mem_spec(mk, 0), mem_spec(mv, v_block),
        small(ng), resident(wc), small(cw), small(cb), small(lg), small(lb),
        resident(wa), resident(wo), small(fg),
    ]
    out_shape = (
        jax.ShapeDtypeStruct((n_streams, seq, D_MODEL), F32),
        jax.ShapeDtypeStruct((1, n_streams, HIST, CONV_CH), F32),
        jax.ShapeDtypeStruct((1, n_streams, WINDOW, ATTN_KV), F32),
        jax.ShapeDtypeStruct((1, n_streams, WINDOW, ATTN_KV), F32),
    )
    out_specs = (
        pl.BlockSpec((nb, t, D_MODEL), stream_tile),
        pl.BlockSpec((1, nb, HIST, CONV_CH), stream4),
        pl.BlockSpec((1, nb, WINDOW, ATTN_KV), stream4),
        pl.BlockSpec((1, nb, WINDOW, ATTN_KV), stream4),
    )
    kv_buf = pltpu.VMEM((N_KV, nb, WINDOW + t, ATTN_KV), BF16)
    scratch = [pltpu.VMEM((nb, HIST_PAD + t, CONV_CH), F32), kv_buf, kv_buf, kv_buf, kv_buf]
    return pl.pallas_call(
        functools.partial(_trunk_kernel, nb=nb, t=t, carry=carry),
        grid=grid, in_specs=in_specs, out_specs=out_specs, out_shape=out_shape,
        scratch_shapes=scratch,
        compiler_params=pltpu.CompilerParams(
            dimension_semantics=("arbitrary",) * len(grid), vmem_limit_bytes=VMEM_LIMIT_BYTES),
        name="trunk_prompt" if carry else "trunk_sample",
    )(sinks, x, cos, sin, hist, ck, cv, mk, mv, ng, wc, cw, cb, lg, lb, wa, wo, fg)


def _mem_kv_kernel(mem_ref, w_ref, k_ref, v_ref, kv16_ref):
    kv = _bdot(mem_ref[...], w_ref[0])
    kv16_ref[0] = kv.astype(BF16)
    for h in range(N_XH):
        k_ref[0, :, h, :] = kv[:, h * XHEAD_DIM:(h + 1) * XHEAD_DIM]
        v_ref[0, :, h, :] = kv[:, X_WIDTH + h * XHEAD_DIM:X_WIDTH + (h + 1) * XHEAD_DIM]


def _mem_kv(mem, w):
    rows = mem.shape[0]
    head_shape = (DEPTH, rows, N_XH, XHEAD_DIM)
    head_spec = pl.BlockSpec((1, rows, N_XH, XHEAD_DIM), lambda l: (l, 0, 0, 0))
    return pl.pallas_call(
        _mem_kv_kernel,
        grid=(DEPTH,),
        in_specs=[pl.BlockSpec((rows, D_MODEL), lambda l: (0, 0)),
                  pl.BlockSpec((1, D_MODEL, 2 * X_WIDTH), lambda l: (l, 0, 0))],
        out_specs=(head_spec, head_spec, pl.BlockSpec((1, rows, 2 * X_WIDTH), lambda l: (l, 0, 0))),
        out_shape=(jax.ShapeDtypeStruct(head_shape, F32), jax.ShapeDtypeStruct(head_shape, F32),
                   jax.ShapeDtypeStruct((DEPTH, rows, 2 * X_WIDTH), BF16)),
        compiler_params=pltpu.CompilerParams(dimension_semantics=("arbitrary",)),
        name="mem_kv",
    )(mem, w)


def _bf16_weight(w):
    w = w.astype(BF16)
    if (w.shape[-1] // LANES) % SUBLANES == 0:
        w = jnp.pad(w, [(0, 0)] * (w.ndim - 1) + [(0, LANES)])
    return w


def _rope_tables(pos):
    half = HEAD_DIM // 2
    inv = ROPE_THETA ** (-jnp.arange(half, dtype=F32) / half)
    ang = pos.astype(F32)[:, None] * inv[None, :]
    cos, sin = jnp.cos(ang), jnp.sin(ang)
    reps = LANES // HEAD_DIM
    return jnp.tile(cos, (1, 2 * reps)), jnp.tile(jnp.concatenate([-sin, sin], axis=1), (1, reps))


PROMPT_TILE = 256
SAMPLE_STREAMS = 4


def kernel(x_prompt, x_sample, state_conv, cache_swa_k, cache_swa_v, cache_mem_k, cache_mem_v, mem_prompt, norm_g, w_in_conv, conv_w, conv_b, ln_g, ln_b, w_in_attn, sinks, w_mem_kv, w_out, final_g):
    batch, seq, _ = x_prompt.shape
    dec_batch, dec_seq, _ = x_sample.shape
    wc = _bf16_weight(w_in_conv)
    wa = _bf16_weight(w_in_attn)
    wo = _bf16_weight(w_out)
    fg = final_g.reshape(1, D_MODEL)
    sinks1 = sinks.reshape(N_Q)
    weights = (sinks1, norm_g, wc, conv_w, conv_b, ln_g, ln_b, wa, wo, fg)

    mem_k_p, mem_v_p, kv16 = _mem_kv(mem_prompt.reshape(batch * N_MEM, D_MODEL).astype(BF16),
                                     w_mem_kv.astype(BF16))
    kv16 = kv16.reshape(DEPTH, batch, N_MEM, 2 * X_WIDTH)
    mem_k_p = mem_k_p.reshape(DEPTH, batch, N_MEM, N_XH, XHEAD_DIM)
    mem_v_p = mem_v_p.reshape(DEPTH, batch, N_MEM, N_XH, XHEAD_DIM)

    cos_p, sin_p = _rope_tables(jnp.arange(seq, dtype=jnp.int32))
    y_p, conv_p, k_p, v_p = _trunk_call(
        x_prompt, cos_p, sin_p,
        jnp.zeros((1, batch, HIST, CONV_CH), F32),
        jnp.zeros((1, batch, WINDOW, ATTN_KV), F32), jnp.zeros((1, batch, WINDOW, ATTN_KV), F32),
        kv16, kv16, 1, *weights,
        nb=1, t=min(PROMPT_TILE, seq), carry=True)

    cos_s, sin_s = _rope_tables(PAST_LEN + jnp.arange(dec_seq, dtype=jnp.int32))
    y_s, conv_s, k_s, v_s = _trunk_call(
        x_sample, cos_s, sin_s, state_conv,
        cache_swa_k.reshape(1, dec_batch, WINDOW, ATTN_KV), cache_swa_v.reshape(1, dec_batch, WINDOW, ATTN_KV),
        cache_mem_k, cache_mem_v,
        0, *weights, nb=SAMPLE_STREAMS, t=dec_seq, carry=False)

    kv_shape = lambda n: (1, n, WINDOW, N_KV, HEAD_DIM)
    return (y_p, y_s, conv_p, conv_s,
            k_p.reshape(kv_shape(batch)), v_p.reshape(kv_shape(batch)),
            k_s.reshape(kv_shape(dec_batch)), v_s.reshape(kv_shape(dec_batch)),
            mem_k_p, mem_v_p)
```

```python
import functools
import math

import jax
import jax.numpy as jnp
from jax import lax
from jax.experimental import pallas as pl
from jax.experimental.pallas import tpu as pltpu

D_MODEL = 1024
DEPTH = 2
CHUNK = 64
N_MEM = 256
CONV_CH = D_MODEL
CONV_WIDTH = 31
HIST = CONV_WIDTH - 1
N_Q = 16
N_KV = 2
HEAD_DIM = 64
GROUP = N_Q // N_KV
WINDOW = 128
ROPE_THETA = 10000.0
N_XH = 4
XHEAD_DIM = 128
X_WIDTH = N_XH * XHEAD_DIM
ATTN_Q = N_Q * HEAD_DIM
ATTN_KV = N_KV * HEAD_DIM
CONV_IN = 3 * CONV_CH + 2 * X_WIDTH
ATTN_IN = 2 * ATTN_Q + 2 * ATTN_KV + 2 * X_WIDTH
BRANCH_W = CONV_CH + X_WIDTH
RMS_EPS = 1e-6
LN_EPS = 1e-5
PAST_LEN = 2048

LANES = 128
SUBLANES = 8
HIST_PAD = 32
KEY_SPAN = WINDOW + CHUNK
PAIRS = GROUP // 2
CONV_BLOCK = 256
VMEM_LIMIT_BYTES = 60 * 1024 * 1024
LOG2E = math.log2(math.e)

F32 = jnp.float32
BF16 = jnp.bfloat16


def _bdot(a, b):
    return jnp.dot(a, b, preferred_element_type=F32)


def _bdot_nt(a, b):
    return lax.dot_general(a, b, (((1,), (1,)), ((), ())), preferred_element_type=F32)


def _rms(x, g):
    return x * lax.rsqrt(jnp.mean(x * x, axis=-1, keepdims=True) + RMS_EPS) * g


def _layer_norm(x, g, b):
    xc = x - jnp.mean(x, axis=-1, keepdims=True)
    return xc * lax.rsqrt(jnp.mean(xc * xc, axis=-1, keepdims=True) + LN_EPS) * g + b


def _silu(x):
    return x * jax.nn.sigmoid(x)


def _rope_lanes(table_t, signs, rows):
    tiled = jnp.concatenate([table_t if sg > 0 else -table_t for sg in signs], axis=0)
    return tiled.T[0:rows]


def _rope(x, cos, sin_signed):
    lane = lax.broadcasted_iota(jnp.int32, x.shape, 1)
    first_half = (lane % HEAD_DIM) < (HEAD_DIM // 2)
    rot = jnp.where(first_half, pltpu.roll(x, LANES - HEAD_DIM // 2, axis=1),
                    pltpu.roll(x, HEAD_DIM // 2, axis=1))
    return x * cos + rot * sin_signed


def _fill_kv(buf_a, buf_b, s, row0, val):
    rows = val.shape[0]
    lane = lax.broadcasted_iota(jnp.int32, val.shape, 1)
    lo = lane < HEAD_DIM
    swapped = pltpu.roll(val, HEAD_DIM, axis=1)
    zero = jnp.zeros_like(val)
    buf_a[0, s, row0:row0 + rows, :] = jnp.where(lo, val, zero).astype(BF16)
    buf_b[0, s, row0:row0 + rows, :] = jnp.where(lo, zero, swapped).astype(BF16)
    buf_a[1, s, row0:row0 + rows, :] = jnp.where(lo, swapped, zero).astype(BF16)
    buf_b[1, s, row0:row0 + rows, :] = jnp.where(lo, zero, val).astype(BF16)


def _mem_slab(ref, layer, s):
    if len(ref.shape) == 5:
        return jnp.concatenate([ref[layer, s, :, h, :] for h in range(N_XH)], axis=1)
    return ref[layer, s]


def _mem_attn(xq, mk, mv):
    outs = []
    for h in range(N_XH):
        sl = slice(h * XHEAD_DIM, (h + 1) * XHEAD_DIM)
        s = _bdot_nt(xq[:, sl].astype(BF16), mk[:, sl].astype(BF16)) * (LOG2E / math.sqrt(XHEAD_DIM))
        p = jnp.exp2(s - jnp.max(s, axis=-1, keepdims=True))
        den = jnp.sum(p, axis=-1, keepdims=True)
        outs.append(_bdot(p.astype(BF16), mv[:, sl].astype(BF16)) * (1.0 / den))
    return jnp.concatenate(outs, axis=1)


def _swa_chunk(q_c, ka, kb, va, vb, s, w0, bias, sinks_ref):
    lane = lax.broadcasted_iota(jnp.int32, (CHUNK, LANES), 1)
    lo = lane < HEAD_DIM
    outs = []
    for h in range(N_KV):
        base = h * PAIRS
        qs = jnp.concatenate([q_c[:, (base + j) * LANES:(base + j + 1) * LANES] for j in range(PAIRS)],
                             axis=0).astype(BF16)
        s_even = _bdot_nt(qs, ka[h, s, w0:w0 + KEY_SPAN, :])
        s_odd = _bdot_nt(qs, kb[h, s, w0:w0 + KEY_SPAN, :])
        if bias is not None:
            s_even = s_even + bias
            s_odd = s_odd + bias
        probs = [[], []]
        inv = [[], []]
        for j in range(PAIRS):
            for par, s_all in enumerate((s_even, s_odd)):
                sj = s_all[j * CHUNK:(j + 1) * CHUNK]
                sink = sinks_ref[h * GROUP + 2 * j + par] * LOG2E
                m = jnp.maximum(jnp.max(sj, axis=-1, keepdims=True), sink)
                p = jnp.exp2(sj - m)
                den = jnp.sum(p, axis=-1, keepdims=True) + jnp.exp2(sink - m)
                probs[par].append(p.astype(BF16))
                inv[par].append(1.0 / den)
        o_pair = (_bdot(jnp.concatenate(probs[0], axis=0), va[h, s, w0:w0 + KEY_SPAN, :])
                  + _bdot(jnp.concatenate(probs[1], axis=0), vb[h, s, w0:w0 + KEY_SPAN, :]))
        for j in range(PAIRS):
            scale = jnp.where(lo, inv[0][j], inv[1][j])
            outs.append(o_pair[j * CHUNK:(j + 1) * CHUNK] * scale)
    return jnp.concatenate(outs, axis=1)


def _trunk_kernel(sinks_ref, x_ref, cos_ref, sin_ref, hist_ref, ck_ref, cv_ref, mk_ref, mv_ref,
                  ng_ref, wc_ref, cw_ref, cb_ref, lg_ref, lb_ref, wa_ref, wo_ref, fg_ref,
                  y_ref, convo_ref, ko_ref, vo_ref,
                  uext, ka, kb, va, vb, *, nb, t, carry):
    m = nb * t
    n_chunk = t // CHUNK
    tile = pl.program_id(1) if carry else None

    def load_history():
        uext[:, HIST_PAD - HIST:HIST_PAD, :] = hist_ref[0]
        for s in range(nb):
            _fill_kv(ka, kb, s, 0, ck_ref[0, s])
            _fill_kv(va, vb, s, 0, cv_ref[0, s])

    if carry:
        pl.when(tile == 0)(load_history)
    else:
        load_history()

    x = x_ref[...].reshape(m, D_MODEL)

    h = _rms(x, ng_ref[0:1, :]).astype(BF16)
    conv_blocks = []
    for c0 in range(0, CONV_CH, CONV_BLOCK):
        cs = slice(c0, c0 + CONV_BLOCK)
        u = (_bdot(h, wc_ref[0, :, c0:c0 + CONV_BLOCK])
             * jax.nn.sigmoid(_bdot(h, wc_ref[0, :, CONV_CH + c0:CONV_CH + c0 + CONV_BLOCK])))
        uext[:, HIST_PAD:HIST_PAD + t, cs] = u.reshape(nb, t, CONV_BLOCK)
        acc = jnp.broadcast_to(cb_ref[:, cs].reshape(1, 1, CONV_BLOCK), (nb, t, CONV_BLOCK))
        for res in range(SUBLANES):
            taps = [k for k in range(CONV_WIDTH) if (HIST_PAD - HIST + k) % SUBLANES == res]
            lo_row = HIST_PAD - HIST + taps[0]
            hi_row = HIST_PAD - HIST + taps[-1] + t
            shifted = uext[:, lo_row:hi_row, cs]
            part = None
            for k in taps:
                off = HIST_PAD - HIST + k - lo_row
                term = shifted[:, off:off + t, :] * cw_ref[0, k:k + 1, cs].reshape(1, 1, CONV_BLOCK)
                part = term if part is None else part + term
            acc = acc + part
        conv_blocks.append(acc.reshape(m, CONV_BLOCK))
    convo_ref[0] = uext[:, HIST_PAD + t - HIST:HIST_PAD + t, :]
    c = jnp.concatenate(conv_blocks, axis=1)
    z = _bdot(h, wc_ref[0, :, 2 * CONV_CH:3 * CONV_CH])
    mix = _silu(_layer_norm(c, lg_ref[...], lb_ref[...])) * _silu(z)
    xq = _bdot(h, wc_ref[0, :, 3 * CONV_CH:3 * CONV_CH + X_WIDTH])
    xz = _bdot(h, wc_ref[0, :, 3 * CONV_CH + X_WIDTH:CONV_IN])
    xo = jnp.concatenate(
        [_mem_attn(xq[s * t:(s + 1) * t], _mem_slab(mk_ref, 0, s), _mem_slab(mv_ref, 0, s))
         for s in range(nb)], axis=0) * _silu(xz)
    x = x + _bdot(jnp.concatenate([mix.astype(BF16), xo.astype(BF16)], axis=1), wo_ref[0, :, 0:D_MODEL])

    h = _rms(x, ng_ref[1:2, :]).astype(BF16)
    cos = jnp.concatenate([_rope_lanes(cos_ref[...], (1, 1, 1, 1), t)] * nb, axis=0)
    sin = jnp.concatenate([_rope_lanes(sin_ref[...], (-1, 1, -1, 1), t)] * nb, axis=0)
    q = _bdot(h, wa_ref[0, :, 0:ATTN_Q])
    q = jnp.concatenate(
        [_rope(q[:, g * LANES:(g + 1) * LANES], cos, sin) for g in range(ATTN_Q // LANES)], axis=1
    ) * (LOG2E / math.sqrt(HEAD_DIM))
    k = _rope(_bdot(h, wa_ref[0, :, ATTN_Q:ATTN_Q + ATTN_KV]), cos, sin)
    v = _bdot(h, wa_ref[0, :, ATTN_Q + ATTN_KV:ATTN_Q + 2 * ATTN_KV])
    for s in range(nb):
        k_s = k[s * t:(s + 1) * t]
        v_s = v[s * t:(s + 1) * t]
        _fill_kv(ka, kb, s, WINDOW, k_s)
        _fill_kv(va, vb, s, WINDOW, v_s)
        if t >= WINDOW:
            ko_ref[0, s] = k_s[t - WINDOW:]
            vo_ref[0, s] = v_s[t - WINDOW:]
        else:
            ko_ref[0, s] = jnp.concatenate([ck_ref[0, s, t:WINDOW, :], k_s], axis=0)
            vo_ref[0, s] = jnp.concatenate([cv_ref[0, s, t:WINDOW, :], v_s], axis=0)

    col = lax.broadcasted_iota(jnp.int32, (1, KEY_SPAN), 1)
    o_rows = []
    for s in range(nb):
        for i in range(n_chunk):
            bias = None
            if carry and i * CHUNK < WINDOW:
                first_valid = jnp.where(tile == 0, WINDOW - i * CHUNK, 0)
                bias = jnp.where(col >= first_valid, 0.0, -jnp.inf).astype(F32)
            r0 = s * t + i * CHUNK
            o_rows.append(_swa_chunk(q[r0:r0 + CHUNK], ka, kb, va, vb, s, i * CHUNK, bias, sinks_ref))
    o = jnp.concatenate(o_rows, axis=0)
    z = _bdot(h, wa_ref[0, :, ATTN_Q + 2 * ATTN_KV:2 * ATTN_Q + 2 * ATTN_KV])
    mix = o * _silu(z)
    xq = _bdot(h, wa_ref[0, :, 2 * ATTN_Q + 2 * ATTN_KV:2 * ATTN_Q + 2 * ATTN_KV + X_WIDTH])
    xz = _bdot(h, wa_ref[0, :, 2 * ATTN_Q + 2 * ATTN_KV + X_WIDTH:ATTN_IN])
    xo = jnp.concatenate(
        [_mem_attn(xq[s * t:(s + 1) * t], _mem_slab(mk_ref, 1, s), _mem_slab(mv_ref, 1, s))
         for s in range(nb)], axis=0) * _silu(xz)
    x = x + _bdot(jnp.concatenate([mix.astype(BF16), xo.astype(BF16)], axis=1), wo_ref[1, :, 0:D_MODEL])

    y_ref[...] = _rms(x, fg_ref[...]).reshape(nb, t, D_MODEL)

    if carry:
        uext[:, 0:HIST_PAD, :] = uext[:, t:t + HIST_PAD, :]
        for buf in (ka, kb, va, vb):
            buf[:, :, 0:WINDOW, :] = buf[:, :, t:t + WINDOW, :]


def _trunk_call(x, cos, sin, hist, ck, cv, mk, mv, v_block, sinks, ng, wc, cw, cb, lg, lb, wa, wo, fg,
                *, nb, t, carry):
    n_streams, seq, _ = x.shape
    if carry:
        assert nb == 1 and seq % t == 0 and t >= WINDOW and t % CHUNK == 0
        grid = (n_streams, seq // t)
        per_stream = lambda b, i: b
        per_tile = lambda b, i: i
    else:
        assert seq == t and n_streams % nb == 0 and t % CHUNK == 0
        grid = (n_streams // nb,)
        per_stream = lambda g: g
        per_tile = lambda g: 0

    def stream_tile(*idx):
        return (per_stream(*idx), per_tile(*idx), 0)

    def stream4(*idx):
        return (0, per_stream(*idx), 0, 0)

    def tile2(*idx):
        return (0, per_tile(*idx))

    pos_block = -(-t // LANES) * LANES
    assert cos.shape == sin.shape == (HEAD_DIM // 2, pos_block * (seq // t))

    def const(rank):
        return lambda *idx: (0,) * rank

    def resident(arr):
        return pl.BlockSpec(arr.shape, const(arr.ndim), pipeline_mode=pl.Buffered(1))

    def small(arr):
        return pl.BlockSpec(arr.shape, const(arr.ndim))

    def mem_spec(arr, last_block):
        if arr.ndim == 5:
            return pl.BlockSpec((DEPTH, nb, N_MEM, N_XH, XHEAD_DIM), lambda *idx: (0, per_stream(*idx), 0, 0, 0),
                                pipeline_mode=pl.Buffered(1))
        return pl.BlockSpec((DEPTH, nb, N_MEM, X_WIDTH), lambda *idx: (0, per_stream(*idx), 0, last_block),
                            pipeline_mode=pl.Buffered(1))

    in_specs = [
        pl.BlockSpec(memory_space=pltpu.SMEM),
        pl.BlockSpec((nb, t, D_MODEL), stream_tile),
        pl.BlockSpec((HEAD_DIM // 2, pos_block), tile2),
        pl.BlockSpec((HEAD_DIM // 2, pos_block), tile2),
        pl.BlockSpec((1, nb, HIST, CONV_CH), stream4),
        pl.BlockSpec((1, nb, WINDOW, ATTN_KV), stream4),
        pl.BlockSpec((1, nb, WINDOW, ATTN_KV), stream4),
        mem_spec(mk, 0), mem_spec(mv, v_block),
        small(ng), resident(wc), small(cw), small(cb), small(lg), small(lb),
        resident(wa), resident(wo), small(fg),
    ]
    out_shape = (
        jax.ShapeDtypeStruct((n_streams, seq, D_MODEL), F32),
        jax.ShapeDtypeStruct((1, n_streams, HIST, CONV_CH), F32),
        jax.ShapeDtypeStruct((1, n_streams, WINDOW, ATTN_KV), F32),
        jax.ShapeDtypeStruct((1, n_streams, WINDOW, ATTN_KV), F32),
    )
    out_specs = (
        pl.BlockSpec((nb, t, D_MODEL), stream_tile),
        pl.BlockSpec((1, nb, HIST, CONV_CH), stream4),
        pl.BlockSpec((1, nb, WINDOW, ATTN_KV), stream4),
        pl.BlockSpec((1, nb, WINDOW, ATTN_KV), stream4),
    )
    kv_buf = pltpu.VMEM((N_KV, nb, WINDOW + t, ATTN_KV), BF16)
    scratch = [pltpu.VMEM((nb, HIST_PAD + t, CONV_CH), F32), kv_buf, kv_buf, kv_buf, kv_buf]
    return pl.pallas_call(
        functools.partial(_trunk_kernel, nb=nb, t=t, carry=carry),
        grid=grid, in_specs=in_specs, out_specs=out_specs, out_shape=out_shape,
        scratch_shapes=scratch,
        compiler_params=pltpu.CompilerParams(
            dimension_semantics=("arbitrary",) * len(grid), vmem_limit_bytes=VMEM_LIMIT_BYTES),
        name="trunk_prompt" if carry else "trunk_sample",
    )(sinks, x, cos, sin, hist, ck, cv, mk, mv, ng, wc, cw, cb, lg, lb, wa, wo, fg)


def _mem_kv_kernel(mem_ref, w_ref, k_ref, v_ref, kv16_ref):
    kv = _bdot(mem_ref[...], w_ref[0])
    kv16_ref[0] = kv.astype(BF16)
    for h in range(N_XH):
        k_ref[0, :, h, :] = kv[:, h * XHEAD_DIM:(h + 1) * XHEAD_DIM]
        v_ref[0, :, h, :] = kv[:, X_WIDTH + h * XHEAD_DIM:X_WIDTH + (h + 1) * XHEAD_DIM]


def _mem_kv(mem, w):
    rows = mem.shape[0]
    head_shape = (DEPTH, rows, N_XH, XHEAD_DIM)
    head_spec = pl.BlockSpec((1, rows, N_XH, XHEAD_DIM), lambda l: (l, 0, 0, 0))
    return pl.pallas_call(
        _mem_kv_kernel,
        grid=(DEPTH,),
        in_specs=[pl.BlockSpec((rows, D_MODEL), lambda l: (0, 0)),
                  pl.BlockSpec((1, D_MODEL, 2 * X_WIDTH), lambda l: (l, 0, 0))],
        out_specs=(head_spec, head_spec, pl.BlockSpec((1, rows, 2 * X_WIDTH), lambda l: (l, 0, 0))),
        out_shape=(jax.ShapeDtypeStruct(head_shape, F32), jax.ShapeDtypeStruct(head_shape, F32),
                   jax.ShapeDtypeStruct((DEPTH, rows, 2 * X_WIDTH), BF16)),
        compiler_params=pltpu.CompilerParams(dimension_semantics=("arbitrary",)),
        name="mem_kv",
    )(mem, w)


def _bf16_weight(w):
    w = w.astype(BF16)
    if (w.shape[-1] // LANES) % SUBLANES == 0:
        w = jnp.pad(w, [(0, 0)] * (w.ndim - 1) + [(0, LANES)])
    return w


def _rope_tables(pos):
    half = HEAD_DIM // 2
    inv = ROPE_THETA ** (-jnp.arange(half, dtype=F32) / half)
    ang = inv[:, None] * pos.astype(F32)[None, :]
    return jnp.cos(ang), jnp.sin(ang)


PROMPT_TILE = 256
SAMPLE_STREAMS = 4


def kernel(x_prompt, x_sample, state_conv, cache_swa_k, cache_swa_v, cache_mem_k, cache_mem_v, mem_prompt, norm_g, w_in_conv, conv_w, conv_b, ln_g, ln_b, w_in_attn, sinks, w_mem_kv, w_out, final_g):
    batch, seq, _ = x_prompt.shape
    dec_batch, dec_seq, _ = x_sample.shape
    wc = _bf16_weight(w_in_conv)
    wa = _bf16_weight(w_in_attn)
    wo = _bf16_weight(w_out)
    fg = final_g.reshape(1, D_MODEL)
    sinks1 = sinks.reshape(N_Q)
    weights = (sinks1, norm_g, wc, conv_w, conv_b, ln_g, ln_b, wa, wo, fg)

    mem_k_p, mem_v_p, kv16 = _mem_kv(mem_prompt.reshape(batch * N_MEM, D_MODEL).astype(BF16),
                                     w_mem_kv.astype(BF16))
    kv16 = kv16.reshape(DEPTH, batch, N_MEM, 2 * X_WIDTH)
    mem_k_p = mem_k_p.reshape(DEPTH, batch, N_MEM, N_XH, XHEAD_DIM)
    mem_v_p = mem_v_p.reshape(DEPTH, batch, N_MEM, N_XH, XHEAD_DIM)

    cos_p, sin_p = _rope_tables(jnp.arange(seq, dtype=jnp.int32))
    y_p, conv_p, k_p, v_p = _trunk_call(
        x_prompt, cos_p, sin_p,
        jnp.zeros((1, batch, HIST, CONV_CH), F32),
        jnp.zeros((1, batch, WINDOW, ATTN_KV), F32), jnp.zeros((1, batch, WINDOW, ATTN_KV), F32),
        kv16, kv16, 1, *weights,
        nb=1, t=min(PROMPT_TILE, seq), carry=True)

    cos_s, sin_s = _rope_tables(PAST_LEN + jnp.arange(-(-dec_seq // LANES) * LANES, dtype=jnp.int32))
    y_s, conv_s, k_s, v_s = _trunk_call(
        x_sample, cos_s, sin_s, state_conv,
        cache_swa_k.reshape(1, dec_batch, WINDOW, ATTN_KV), cache_swa_v.reshape(1, dec_batch, WINDOW, ATTN_KV),
        cache_mem_k, cache_mem_v,
        0, *weights, nb=SAMPLE_STREAMS, t=dec_seq, carry=False)

    kv_shape = lambda n: (1, n, WINDOW, N_KV, HEAD_DIM)
    return (y_p, y_s, conv_p, conv_s,
            k_p.reshape(kv_shape(batch)), v_p.reshape(kv_shape(batch)),
            k_s.reshape(kv_shape(dec_batch)), v_s.reshape(kv_shape(dec_batch)),
            mem_k_p, mem_v_p)
```

```python
import functools
import math

import jax
import jax.numpy as jnp
from jax import lax
from jax.experimental import pallas as pl
from jax.experimental.pallas import tpu as pltpu

D_MODEL = 1024
DEPTH = 2
CHUNK = 64
N_MEM = 256
CONV_CH = D_MODEL
CONV_WIDTH = 31
HIST = CONV_WIDTH - 1
N_Q = 16
N_KV = 2
HEAD_DIM = 64
GROUP = N_Q // N_KV
WINDOW = 128
ROPE_THETA = 10000.0
N_XH = 4
XHEAD_DIM = 128
X_WIDTH = N_XH * XHEAD_DIM
ATTN_Q = N_Q * HEAD_DIM
ATTN_KV = N_KV * HEAD_DIM
CONV_IN = 3 * CONV_CH + 2 * X_WIDTH
ATTN_IN = 2 * ATTN_Q + 2 * ATTN_KV + 2 * X_WIDTH
BRANCH_W = CONV_CH + X_WIDTH
RMS_EPS = 1e-6
LN_EPS = 1e-5
PAST_LEN = 2048

LANES = 128
SUBLANES = 8
HIST_PAD = 32
KEY_SPAN = WINDOW + CHUNK
PAIRS = GROUP // 2
CONV_BLOCK = 256
VMEM_LIMIT_BYTES = 60 * 1024 * 1024
LOG2E = math.log2(math.e)

F32 = jnp.float32
BF16 = jnp.bfloat16


def _bdot(a, b):
    return jnp.dot(a, b, preferred_element_type=F32)


def _bdot_nt(a, b):
    return lax.dot_general(a, b, (((1,), (1,)), ((), ())), preferred_element_type=F32)


def _rms(x, g):
    return x * lax.rsqrt(jnp.mean(x * x, axis=-1, keepdims=True) + RMS_EPS) * g


def _layer_norm(x, g, b):
    xc = x - jnp.mean(x, axis=-1, keepdims=True)
    return xc * lax.rsqrt(jnp.mean(xc * xc, axis=-1, keepdims=True) + LN_EPS) * g + b


def _silu(x):
    return x * jax.nn.sigmoid(x)


def _rope_lanes(table_t, signs, rows):
    tiled = jnp.concatenate([table_t if sg > 0 else -table_t for sg in signs], axis=0)
    return tiled.T[0:rows]


def _rope(x, cos, sin_signed):
    lane = lax.broadcasted_iota(jnp.int32, x.shape, 1)
    first_half = (lane % HEAD_DIM) < (HEAD_DIM // 2)
    rot = jnp.where(first_half, pltpu.roll(x, LANES - HEAD_DIM // 2, axis=1),
                    pltpu.roll(x, HEAD_DIM // 2, axis=1))
    return x * cos + rot * sin_signed


def _fill_kv(buf_a, buf_b, s, row0, val):
    rows = val.shape[0]
    lane = lax.broadcasted_iota(jnp.int32, val.shape, 1)
    lo = lane < HEAD_DIM
    swapped = pltpu.roll(val, HEAD_DIM, axis=1)
    zero = jnp.zeros_like(val)
    buf_a[0, s, row0:row0 + rows, :] = jnp.where(lo, val, zero).astype(BF16)
    buf_b[0, s, row0:row0 + rows, :] = jnp.where(lo, zero, swapped).astype(BF16)
    buf_a[1, s, row0:row0 + rows, :] = jnp.where(lo, swapped, zero).astype(BF16)
    buf_b[1, s, row0:row0 + rows, :] = jnp.where(lo, zero, val).astype(BF16)


def _mem_slab(ref, layer, s):
    if len(ref.shape) == 5:
        return jnp.concatenate([ref[layer, s, :, h, :] for h in range(N_XH)], axis=1)
    return ref[layer, s]


def _mem_attn(xq, mk, mv):
    outs = []
    for h in range(N_XH):
        sl = slice(h * XHEAD_DIM, (h + 1) * XHEAD_DIM)
        q_h = (xq[:, sl] * (LOG2E / math.sqrt(XHEAD_DIM))).astype(BF16)
        s = _bdot_nt(q_h, mk[:, sl].astype(BF16))
        p = jnp.exp2(s - jnp.max(s, axis=-1, keepdims=True))
        den = jnp.sum(p, axis=-1, keepdims=True)
        outs.append(_bdot(p.astype(BF16), mv[:, sl].astype(BF16)) * (1.0 / den))
    return jnp.concatenate(outs, axis=1)


def _swa_chunk(q_c, ka, kb, va, vb, s, w0, bias, sinks_ref):
    lane = lax.broadcasted_iota(jnp.int32, (CHUNK, LANES), 1)
    lo = lane < HEAD_DIM
    outs = []
    for h in range(N_KV):
        base = h * PAIRS
        qs = jnp.concatenate([q_c[:, (base + j) * LANES:(base + j + 1) * LANES] for j in range(PAIRS)],
                             axis=0).astype(BF16)
        s_even = _bdot_nt(qs, ka[h, s, w0:w0 + KEY_SPAN, :])
        s_odd = _bdot_nt(qs, kb[h, s, w0:w0 + KEY_SPAN, :])
        if bias is not None:
            s_even = s_even + bias
            s_odd = s_odd + bias
        probs = [[], []]
        inv = [[], []]
        for j in range(PAIRS):
            for par, s_all in enumerate((s_even, s_odd)):
                sj = s_all[j * CHUNK:(j + 1) * CHUNK]
                sink = sinks_ref[h * GROUP + 2 * j + par] * LOG2E
                m = jnp.maximum(jnp.max(sj, axis=-1, keepdims=True), sink)
                p = jnp.exp2(sj - m)
                den = jnp.sum(p, axis=-1, keepdims=True) + jnp.exp2(sink - m)
                probs[par].append(p.astype(BF16))
                inv[par].append(1.0 / den)
        o_pair = (_bdot(jnp.concatenate(probs[0], axis=0), va[h, s, w0:w0 + KEY_SPAN, :])
                  + _bdot(jnp.concatenate(probs[1], axis=0), vb[h, s, w0:w0 + KEY_SPAN, :]))
        for j in range(PAIRS):
            scale = jnp.where(lo, inv[0][j], inv[1][j])
            outs.append(o_pair[j * CHUNK:(j + 1) * CHUNK] * scale)
    return jnp.concatenate(outs, axis=1)


def _trunk_kernel(sinks_ref, x_ref, cos_ref, sin_ref, hist_ref, ck_ref, cv_ref, mk_ref, mv_ref,
                  ng_ref, wc_ref, cw_ref, cb_ref, lg_ref, lb_ref, wa_ref, wo_ref, fg_ref,
                  y_ref, convo_ref, ko_ref, vo_ref,
                  uext, ka, kb, va, vb, *, nb, t, carry):
    m = nb * t
    n_chunk = t // CHUNK
    tile = pl.program_id(1) if carry else None

    def load_history():
        uext[:, HIST_PAD - HIST:HIST_PAD, :] = hist_ref[0]
        for s in range(nb):
            _fill_kv(ka, kb, s, 0, ck_ref[0, s])
            _fill_kv(va, vb, s, 0, cv_ref[0, s])

    if carry:
        pl.when(tile == 0)(load_history)
    else:
        load_history()

    x = x_ref[...].reshape(m, D_MODEL)

    h = _rms(x, ng_ref[0:1, :]).astype(BF16)
    conv_blocks = []
    for c0 in range(0, CONV_CH, CONV_BLOCK):
        cs = slice(c0, c0 + CONV_BLOCK)
        u = (_bdot(h, wc_ref[0, :, c0:c0 + CONV_BLOCK])
             * jax.nn.sigmoid(_bdot(h, wc_ref[0, :, CONV_CH + c0:CONV_CH + c0 + CONV_BLOCK])))
        uext[:, HIST_PAD:HIST_PAD + t, cs] = u.reshape(nb, t, CONV_BLOCK)
        acc = jnp.broadcast_to(cb_ref[:, cs].reshape(1, 1, CONV_BLOCK), (nb, t, CONV_BLOCK))
        for res in range(SUBLANES):
            taps = [k for k in range(CONV_WIDTH) if (HIST_PAD - HIST + k) % SUBLANES == res]
            lo_row = HIST_PAD - HIST + taps[0]
            hi_row = HIST_PAD - HIST + taps[-1] + t
            shifted = uext[:, lo_row:hi_row, cs]
            part = None
            for k in taps:
                off = HIST_PAD - HIST + k - lo_row
                term = shifted[:, off:off + t, :] * cw_ref[0, k:k + 1, cs].reshape(1, 1, CONV_BLOCK)
                part = term if part is None else part + term
            acc = acc + part
        conv_blocks.append(acc.reshape(m, CONV_BLOCK))
    convo_ref[0] = uext[:, HIST_PAD + t - HIST:HIST_PAD + t, :]
    c = jnp.concatenate(conv_blocks, axis=1)
    z = _bdot(h, wc_ref[0, :, 2 * CONV_CH:3 * CONV_CH])
    mix = _silu(_layer_norm(c, lg_ref[...], lb_ref[...])) * _silu(z)
    xq = _bdot(h, wc_ref[0, :, 3 * CONV_CH:3 * CONV_CH + X_WIDTH])
    xz = _bdot(h, wc_ref[0, :, 3 * CONV_CH + X_WIDTH:CONV_IN])
    xo = jnp.concatenate(
        [_mem_attn(xq[s * t:(s + 1) * t], _mem_slab(mk_ref, 0, s), _mem_slab(mv_ref, 0, s))
         for s in range(nb)], axis=0) * _silu(xz)
    x = x + _bdot(jnp.concatenate([mix.astype(BF16), xo.astype(BF16)], axis=1), wo_ref[0, :, 0:D_MODEL])

    h = _rms(x, ng_ref[1:2, :]).astype(BF16)
    cos = jnp.concatenate([_rope_lanes(cos_ref[...], (1, 1, 1, 1), t)] * nb, axis=0)
    sin = jnp.concatenate([_rope_lanes(sin_ref[...], (-1, 1, -1, 1), t)] * nb, axis=0)
    q = _bdot(h, wa_ref[0, :, 0:ATTN_Q])
    cos_q = cos * (LOG2E / math.sqrt(HEAD_DIM))
    sin_q = sin * (LOG2E / math.sqrt(HEAD_DIM))
    q = jnp.concatenate(
        [_rope(q[:, g * LANES:(g + 1) * LANES], cos_q, sin_q) for g in range(ATTN_Q // LANES)], axis=1)
    k = _rope(_bdot(h, wa_ref[0, :, ATTN_Q:ATTN_Q + ATTN_KV]), cos, sin)
    v = _bdot(h, wa_ref[0, :, ATTN_Q + ATTN_KV:ATTN_Q + 2 * ATTN_KV])
    for s in range(nb):
        k_s = k[s * t:(s + 1) * t]
        v_s = v[s * t:(s + 1) * t]
        _fill_kv(ka, kb, s, WINDOW, k_s)
        _fill_kv(va, vb, s, WINDOW, v_s)
        if t >= WINDOW:
            ko_ref[0, s] = k_s[t - WINDOW:]
            vo_ref[0, s] = v_s[t - WINDOW:]
        else:
            ko_ref[0, s] = jnp.concatenate([ck_ref[0, s, t:WINDOW, :], k_s], axis=0)
            vo_ref[0, s] = jnp.concatenate([cv_ref[0, s, t:WINDOW, :], v_s], axis=0)

    col = lax.broadcasted_iota(jnp.int32, (1, KEY_SPAN), 1)
    o_rows = []
    for s in range(nb):
        for i in range(n_chunk):
            bias = None
            if carry and i * CHUNK < WINDOW:
                first_valid = jnp.where(tile == 0, WINDOW - i * CHUNK, 0)
                bias = jnp.where(col >= first_valid, 0.0, -jnp.inf).astype(F32)
            r0 = s * t + i * CHUNK
            o_rows.append(_swa_chunk(q[r0:r0 + CHUNK], ka, kb, va, vb, s, i * CHUNK, bias, sinks_ref))
    o = jnp.concatenate(o_rows, axis=0)
    z = _bdot(h, wa_ref[0, :, ATTN_Q + 2 * ATTN_KV:2 * ATTN_Q + 2 * ATTN_KV])
    mix = o * _silu(z)
    xq = _bdot(h, wa_ref[0, :, 2 * ATTN_Q + 2 * ATTN_KV:2 * ATTN_Q + 2 * ATTN_KV + X_WIDTH])
    xz = _bdot(h, wa_ref[0, :, 2 * ATTN_Q + 2 * ATTN_KV + X_WIDTH:ATTN_IN])
    xo = jnp.concatenate(
        [_mem_attn(xq[s * t:(s + 1) * t], _mem_slab(mk_ref, 1, s), _mem_slab(mv_ref, 1, s))
         for s in range(nb)], axis=0) * _silu(xz)
    x = x + _bdot(jnp.concatenate([mix.astype(BF16), xo.astype(BF16)], axis=1), wo_ref[1, :, 0:D_MODEL])

    y_ref[...] = _rms(x, fg_ref[...]).reshape(nb, t, D_MODEL)

    if carry:
        uext[:, 0:HIST_PAD, :] = uext[:, t:t + HIST_PAD, :]
        for buf in (ka, kb, va, vb):
            buf[:, :, 0:WINDOW, :] = buf[:, :, t:t + WINDOW, :]


def _trunk_call(x, cos, sin, hist, ck, cv, mk, mv, v_block, sinks, ng, wc, cw, cb, lg, lb, wa, wo, fg,
                *, nb, t, carry):
    n_streams, seq, _ = x.shape
    if carry:
        assert nb == 1 and seq % t == 0 and t >= WINDOW and t % CHUNK == 0
        grid = (n_streams, seq // t)
        per_stream = lambda b, i: b
        per_tile = lambda b, i: i
    else:
        assert seq == t and n_streams % nb == 0 and t % CHUNK == 0
        grid = (n_streams // nb,)
        per_stream = lambda g: g
        per_tile = lambda g: 0

    def stream_tile(*idx):
        return (per_stream(*idx), per_tile(*idx), 0)

    def stream4(*idx):
        return (0, per_stream(*idx), 0, 0)

    def tile2(*idx):
        return (0, per_tile(*idx))

    pos_block = -(-t // LANES) * LANES
    assert cos.shape == sin.shape == (HEAD_DIM // 2, pos_block * (seq // t))

    def const(rank):
        return lambda *idx: (0,) * rank

    def resident(arr):
        return pl.BlockSpec(arr.shape, const(arr.ndim), pipeline_mode=pl.Buffered(1))

    def small(arr):
        return pl.BlockSpec(arr.shape, const(arr.ndim))

    def mem_spec(arr, last_block):
        if arr.ndim == 5:
            return pl.BlockSpec((DEPTH, nb, N_MEM, N_XH, XHEAD_DIM), lambda *idx: (0, per_stream(*idx), 0, 0, 0))
        return pl.BlockSpec((DEPTH, nb, N_MEM, X_WIDTH), lambda *idx: (0, per_stream(*idx), 0, last_block),
                            pipeline_mode=pl.Buffered(1))

    in_specs = [
        pl.BlockSpec(memory_space=pltpu.SMEM),
        pl.BlockSpec((nb, t, D_MODEL), stream_tile),
        pl.BlockSpec((HEAD_DIM // 2, pos_block), tile2),
        pl.BlockSpec((HEAD_DIM // 2, pos_block), tile2),
        pl.BlockSpec((1, nb, HIST, CONV_CH), stream4),
        pl.BlockSpec((1, nb, WINDOW, ATTN_KV), stream4),
        pl.BlockSpec((1, nb, WINDOW, ATTN_KV), stream4),
        mem_spec(mk, 0), mem_spec(mv, v_block),
        small(ng), resident(wc), small(cw), small(cb), small(lg), small(lb),
        resident(wa), resident(wo), small(fg),
    ]
    out_shape = (
        jax.ShapeDtypeStruct((n_streams, seq, D_MODEL), F32),
        jax.ShapeDtypeStruct((1, n_streams, HIST, CONV_CH), F32),
        jax.ShapeDtypeStruct((1, n_streams, WINDOW, ATTN_KV), F32),
        jax.ShapeDtypeStruct((1, n_streams, WINDOW, ATTN_KV), F32),
    )
    out_specs = (
        pl.BlockSpec((nb, t, D_MODEL), stream_tile),
        pl.BlockSpec((1, nb, HIST, CONV_CH), stream4),
        pl.BlockSpec((1, nb, WINDOW, ATTN_KV), stream4),
        pl.BlockSpec((1, nb, WINDOW, ATTN_KV), stream4),
    )
    kv_buf = pltpu.VMEM((N_KV, nb, WINDOW + t, ATTN_KV), BF16)
    scratch = [pltpu.VMEM((nb, HIST_PAD + t, CONV_CH), F32), kv_buf, kv_buf, kv_buf, kv_buf]
    return pl.pallas_call(
        functools.partial(_trunk_kernel, nb=nb, t=t, carry=carry),
        grid=grid, in_specs=in_specs, out_specs=out_specs, out_shape=out_shape,
        scratch_shapes=scratch,
        compiler_params=pltpu.CompilerParams(
            dimension_semantics=("arbitrary",) * len(grid), vmem_limit_bytes=VMEM_LIMIT_BYTES),
        name="trunk_prompt" if carry else "trunk_sample",
    )(sinks, x, cos, sin, hist, ck, cv, mk, mv, ng, wc, cw, cb, lg, lb, wa, wo, fg)


def _mem_kv_kernel(mem_ref, w_ref, k_ref, v_ref, kv16_ref):
    kv = _bdot(mem_ref[...], w_ref[0])
    kv16_ref[0] = kv.astype(BF16)
    for h in range(N_XH):
        k_ref[0, :, h, :] = kv[:, h * XHEAD_DIM:(h + 1) * XHEAD_DIM]
        v_ref[0, :, h, :] = kv[:, X_WIDTH + h * XHEAD_DIM:X_WIDTH + (h + 1) * XHEAD_DIM]


def _mem_kv(mem, w):
    rows = mem.shape[0]
    head_shape = (DEPTH, rows, N_XH, XHEAD_DIM)
    head_spec = pl.BlockSpec((1, rows, N_XH, XHEAD_DIM), lambda l: (l, 0, 0, 0))
    return pl.pallas_call(
        _mem_kv_kernel,
        grid=(DEPTH,),
        in_specs=[pl.BlockSpec((rows, D_MODEL), lambda l: (0, 0)),
                  pl.BlockSpec((1, D_MODEL, 2 * X_WIDTH), lambda l: (l, 0, 0))],
        out_specs=(head_spec, head_spec, pl.BlockSpec((1, rows, 2 * X_WIDTH), lambda l: (l, 0, 0))),
        out_shape=(jax.ShapeDtypeStruct(head_shape, F32), jax.ShapeDtypeStruct(head_shape, F32),
                   jax.ShapeDtypeStruct((DEPTH, rows, 2 * X_WIDTH), BF16)),
        compiler_params=pltpu.CompilerParams(dimension_semantics=("arbitrary",)),
        name="mem_kv",
    )(mem, w)


def _bf16_weight(w):
    if (w.shape[-1] // LANES) % SUBLANES == 0:
        w = jnp.pad(w, [(0, 0)] * (w.ndim - 1) + [(0, LANES)])
    return w.astype(BF16)


def _rope_tables(pos):
    half = HEAD_DIM // 2
    inv = ROPE_THETA ** (-jnp.arange(half, dtype=F32) / half)
    ang = inv[:, None] * pos.astype(F32)[None, :]
    return jnp.cos(ang), jnp.sin(ang)


PROMPT_TILE = 256
SAMPLE_STREAMS = 2


def kernel(x_prompt, x_sample, state_conv, cache_swa_k, cache_swa_v, cache_mem_k, cache_mem_v, mem_prompt, norm_g, w_in_conv, conv_w, conv_b, ln_g, ln_b, w_in_attn, sinks, w_mem_kv, w_out, final_g):
    batch, seq, _ = x_prompt.shape
    dec_batch, dec_seq, _ = x_sample.shape
    wc = _bf16_weight(w_in_conv)
    wa = _bf16_weight(w_in_attn)
    wo = _bf16_weight(w_out)
    fg = final_g.reshape(1, D_MODEL)
    sinks1 = sinks.reshape(N_Q)
    weights = (sinks1, norm_g, wc, conv_w, conv_b, ln_g, ln_b, wa, wo, fg)

    mem_k_p, mem_v_p, kv16 = _mem_kv(mem_prompt.reshape(batch * N_MEM, D_MODEL).astype(BF16),
                                     w_mem_kv.astype(BF16))
    kv16 = kv16.reshape(DEPTH, batch, N_MEM, 2 * X_WIDTH)
    mem_k_p = mem_k_p.reshape(DEPTH, batch, N_MEM, N_XH, XHEAD_DIM)
    mem_v_p = mem_v_p.reshape(DEPTH, batch, N_MEM, N_XH, XHEAD_DIM)

    cos_p, sin_p = _rope_tables(jnp.arange(seq, dtype=jnp.int32))
    y_p, conv_p, k_p, v_p = _trunk_call(
        x_prompt, cos_p, sin_p,
        jnp.zeros((1, batch, HIST, CONV_CH), F32),
        jnp.zeros((1, batch, WINDOW, ATTN_KV), F32), jnp.zeros((1, batch, WINDOW, ATTN_KV), F32),
        kv16, kv16, 1, *weights,
        nb=1, t=min(PROMPT_TILE, seq), carry=True)

    cos_s, sin_s = _rope_tables(PAST_LEN + jnp.arange(-(-dec_seq // LANES) * LANES, dtype=jnp.int32))
    y_s, conv_s, k_s, v_s = _trunk_call(
        x_sample, cos_s, sin_s, state_conv,
        cache_swa_k.reshape(1, dec_batch, WINDOW, ATTN_KV), cache_swa_v.reshape(1, dec_batch, WINDOW, ATTN_KV),
        cache_mem_k, cache_mem_v,
        0, *weights, nb=SAMPLE_STREAMS, t=dec_seq, carry=False)

    kv_shape = lambda n: (1, n, WINDOW, N_KV, HEAD_DIM)
    return (y_p, y_s, conv_p, conv_s,
            k_p.reshape(kv_shape(batch)), v_p.reshape(kv_shape(batch)),
            k_s.reshape(kv_shape(dec_batch)), v_s.reshape(kv_shape(dec_batch)),
            mem_k_p, mem_v_p)
```

```python
import functools
import math

import jax
import jax.numpy as jnp
from jax import lax
from jax.experimental import pallas as pl
from jax.experimental.pallas import tpu as pltpu

D_MODEL = 1024
DEPTH = 2
CHUNK = 64
N_MEM = 256
CONV_CH = D_MODEL
CONV_WIDTH = 31
HIST = CONV_WIDTH - 1
N_Q = 16
N_KV = 2
HEAD_DIM = 64
GROUP = N_Q // N_KV
WINDOW = 128
ROPE_THETA = 10000.0
N_XH = 4
XHEAD_DIM = 128
X_WIDTH = N_XH * XHEAD_DIM
ATTN_Q = N_Q * HEAD_DIM
ATTN_KV = N_KV * HEAD_DIM
CONV_IN = 3 * CONV_CH + 2 * X_WIDTH
ATTN_IN = 2 * ATTN_Q + 2 * ATTN_KV + 2 * X_WIDTH
BRANCH_W = CONV_CH + X_WIDTH
RMS_EPS = 1e-6
LN_EPS = 1e-5
PAST_LEN = 2048

LANES = 128
SUBLANES = 8
HIST_PAD = 32
KEY_SPAN = WINDOW + CHUNK
PAIRS = GROUP // 2
CONV_BLOCK = 256
VMEM_LIMIT_BYTES = 60 * 1024 * 1024
LOG2E = math.log2(math.e)

F32 = jnp.float32
BF16 = jnp.bfloat16


def _bdot(a, b):
    return jnp.dot(a, b, preferred_element_type=F32)


def _bdot_nt(a, b):
    return lax.dot_general(a, b, (((1,), (1,)), ((), ())), preferred_element_type=F32)


def _rms(x, g):
    return x * lax.rsqrt(jnp.mean(x * x, axis=-1, keepdims=True) + RMS_EPS) * g


def _layer_norm(x, g, b):
    xc = x - jnp.mean(x, axis=-1, keepdims=True)
    return xc * lax.rsqrt(jnp.mean(xc * xc, axis=-1, keepdims=True) + LN_EPS) * g + b


def _silu(x):
    return x * jax.nn.sigmoid(x)


def _rope_lanes(table_t, signs, rows):
    tiled = jnp.concatenate([table_t if sg > 0 else -table_t for sg in signs], axis=0)
    return tiled.T[0:rows]


def _rope(x, cos, sin_signed):
    lane = lax.broadcasted_iota(jnp.int32, x.shape, 1)
    first_half = (lane % HEAD_DIM) < (HEAD_DIM // 2)
    rot = jnp.where(first_half, pltpu.roll(x, LANES - HEAD_DIM // 2, axis=1),
                    pltpu.roll(x, HEAD_DIM // 2, axis=1))
    return x * cos + rot * sin_signed


def _fill_kv(buf_a, buf_b, s, row0, val):
    rows = val.shape[0]
    lane = lax.broadcasted_iota(jnp.int32, val.shape, 1)
    lo = lane < HEAD_DIM
    swapped = pltpu.roll(val, HEAD_DIM, axis=1)
    zero = jnp.zeros_like(val)
    buf_a[0, s, row0:row0 + rows, :] = jnp.where(lo, val, zero).astype(BF16)
    buf_b[0, s, row0:row0 + rows, :] = jnp.where(lo, zero, swapped).astype(BF16)
    buf_a[1, s, row0:row0 + rows, :] = jnp.where(lo, swapped, zero).astype(BF16)
    buf_b[1, s, row0:row0 + rows, :] = jnp.where(lo, zero, val).astype(BF16)


def _mem_slab(ref, layer, s):
    if ref.shape[-1] == XHEAD_DIM:
        return jnp.concatenate([ref[layer, s, pl.ds(h, N_MEM, stride=N_XH), :] for h in range(N_XH)], axis=1)
    return ref[layer, s]


def _mem_attn(xq, mk, mv):
    outs = []
    for h in range(N_XH):
        sl = slice(h * XHEAD_DIM, (h + 1) * XHEAD_DIM)
        q_h = (xq[:, sl] * (LOG2E / math.sqrt(XHEAD_DIM))).astype(BF16)
        s = _bdot_nt(q_h, mk[:, sl].astype(BF16))
        p = jnp.exp2(s - jnp.max(s, axis=-1, keepdims=True))
        den = jnp.sum(p, axis=-1, keepdims=True)
        outs.append(_bdot(p.astype(BF16), mv[:, sl].astype(BF16)) * (1.0 / den))
    return jnp.concatenate(outs, axis=1)


def _swa_chunk(q_c, ka, kb, va, vb, s, w0, bias, sinks_ref):
    lane = lax.broadcasted_iota(jnp.int32, (CHUNK, LANES), 1)
    lo = lane < HEAD_DIM
    outs = []
    for h in range(N_KV):
        base = h * PAIRS
        qs = jnp.concatenate([q_c[:, (base + j) * LANES:(base + j + 1) * LANES] for j in range(PAIRS)],
                             axis=0).astype(BF16)
        s_even = _bdot_nt(qs, ka[h, s, w0:w0 + KEY_SPAN, :])
        s_odd = _bdot_nt(qs, kb[h, s, w0:w0 + KEY_SPAN, :])
        if bias is not None:
            s_even = s_even + bias
            s_odd = s_odd + bias
        probs = [[], []]
        inv = [[], []]
        for j in range(PAIRS):
            for par, s_all in enumerate((s_even, s_odd)):
                sj = s_all[j * CHUNK:(j + 1) * CHUNK]
                sink = sinks_ref[h * GROUP + 2 * j + par] * LOG2E
                m = jnp.maximum(jnp.max(sj, axis=-1, keepdims=True), sink)
                p = jnp.exp2(sj - m)
                den = jnp.sum(p, axis=-1, keepdims=True) + jnp.exp2(sink - m)
                probs[par].append(p.astype(BF16))
                inv[par].append(1.0 / den)
        o_pair = (_bdot(jnp.concatenate(probs[0], axis=0), va[h, s, w0:w0 + KEY_SPAN, :])
                  + _bdot(jnp.concatenate(probs[1], axis=0), vb[h, s, w0:w0 + KEY_SPAN, :]))
        for j in range(PAIRS):
            scale = jnp.where(lo, inv[0][j], inv[1][j])
            outs.append(o_pair[j * CHUNK:(j + 1) * CHUNK] * scale)
    return jnp.concatenate(outs, axis=1)


def _trunk_kernel(sinks_ref, x_ref, cos_ref, sin_ref, hist_ref, ck_ref, cv_ref, mk_ref, mv_ref,
                  ng_ref, wc_ref, cw_ref, cb_ref, lg_ref, lb_ref, wa_ref, wo_ref, fg_ref,
                  y_ref, convo_ref, ko_ref, vo_ref,
                  uext, ka, kb, va, vb, *, nb, t, carry):
    m = nb * t
    n_chunk = t // CHUNK
    tile = pl.program_id(1) if carry else None

    def load_history():
        uext[:, HIST_PAD - HIST:HIST_PAD, :] = hist_ref[0]
        for s in range(nb):
            _fill_kv(ka, kb, s, 0, ck_ref[0, s])
            _fill_kv(va, vb, s, 0, cv_ref[0, s])

    if carry:
        pl.when(tile == 0)(load_history)
    else:
        load_history()

    x = x_ref[...].reshape(m, D_MODEL)

    h = _rms(x, ng_ref[0:1, :]).astype(BF16)
    conv_blocks = []
    for c0 in range(0, CONV_CH, CONV_BLOCK):
        cs = slice(c0, c0 + CONV_BLOCK)
        u = (_bdot(h, wc_ref[0, :, c0:c0 + CONV_BLOCK])
             * jax.nn.sigmoid(_bdot(h, wc_ref[0, :, CONV_CH + c0:CONV_CH + c0 + CONV_BLOCK])))
        uext[:, HIST_PAD:HIST_PAD + t, cs] = u.reshape(nb, t, CONV_BLOCK)
        acc = jnp.broadcast_to(cb_ref[:, cs].reshape(1, 1, CONV_BLOCK), (nb, t, CONV_BLOCK))
        for res in range(SUBLANES):
            taps = [k for k in range(CONV_WIDTH) if (HIST_PAD - HIST + k) % SUBLANES == res]
            lo_row = HIST_PAD - HIST + taps[0]
            hi_row = HIST_PAD - HIST + taps[-1] + t
            shifted = uext[:, lo_row:hi_row, cs]
            part = None
            for k in taps:
                off = HIST_PAD - HIST + k - lo_row
                term = shifted[:, off:off + t, :] * cw_ref[0, k:k + 1, cs].reshape(1, 1, CONV_BLOCK)
                part = term if part is None else part + term
            acc = acc + part
        conv_blocks.append(acc.reshape(m, CONV_BLOCK))
    convo_ref[0] = uext[:, HIST_PAD + t - HIST:HIST_PAD + t, :]
    c = jnp.concatenate(conv_blocks, axis=1)
    z = _bdot(h, wc_ref[0, :, 2 * CONV_CH:3 * CONV_CH])
    mix = _silu(_layer_norm(c, lg_ref[...], lb_ref[...])) * _silu(z)
    xq = _bdot(h, wc_ref[0, :, 3 * CONV_CH:3 * CONV_CH + X_WIDTH])
    xz = _bdot(h, wc_ref[0, :, 3 * CONV_CH + X_WIDTH:CONV_IN])
    xo = jnp.concatenate(
        [_mem_attn(xq[s * t:(s + 1) * t], _mem_slab(mk_ref, 0, s), _mem_slab(mv_ref, 0, s))
         for s in range(nb)], axis=0) * _silu(xz)
    x = x + _bdot(jnp.concatenate([mix.astype(BF16), xo.astype(BF16)], axis=1), wo_ref[0, :, 0:D_MODEL])

    h = _rms(x, ng_ref[1:2, :]).astype(BF16)
    cos = jnp.concatenate([_rope_lanes(cos_ref[...], (1, 1, 1, 1), t)] * nb, axis=0)
    sin = jnp.concatenate([_rope_lanes(sin_ref[...], (-1, 1, -1, 1), t)] * nb, axis=0)
    q = _bdot(h, wa_ref[0, :, 0:ATTN_Q])
    cos_q = cos * (LOG2E / math.sqrt(HEAD_DIM))
    sin_q = sin * (LOG2E / math.sqrt(HEAD_DIM))
    q = jnp.concatenate(
        [_rope(q[:, g * LANES:(g + 1) * LANES], cos_q, sin_q) for g in range(ATTN_Q // LANES)], axis=1)
    k = _rope(_bdot(h, wa_ref[0, :, ATTN_Q:ATTN_Q + ATTN_KV]), cos, sin)
    v = _bdot(h, wa_ref[0, :, ATTN_Q + ATTN_KV:ATTN_Q + 2 * ATTN_KV])
    for s in range(nb):
        k_s = k[s * t:(s + 1) * t]
        v_s = v[s * t:(s + 1) * t]
        _fill_kv(ka, kb, s, WINDOW, k_s)
        _fill_kv(va, vb, s, WINDOW, v_s)
        if t >= WINDOW:
            ko_ref[0, s] = k_s[t - WINDOW:]
            vo_ref[0, s] = v_s[t - WINDOW:]
        else:
            ko_ref[0, s] = jnp.concatenate([ck_ref[0, s, t:WINDOW, :], k_s], axis=0)
            vo_ref[0, s] = jnp.concatenate([cv_ref[0, s, t:WINDOW, :], v_s], axis=0)

    col = lax.broadcasted_iota(jnp.int32, (1, KEY_SPAN), 1)
    o_rows = []
    for s in range(nb):
        for i in range(n_chunk):
            bias = None
            if carry and i * CHUNK < WINDOW:
                first_valid = jnp.where(tile == 0, WINDOW - i * CHUNK, 0)
                bias = jnp.where(col >= first_valid, 0.0, -jnp.inf).astype(F32)
            r0 = s * t + i * CHUNK
            o_rows.append(_swa_chunk(q[r0:r0 + CHUNK], ka, kb, va, vb, s, i * CHUNK, bias, sinks_ref))
    o = jnp.concatenate(o_rows, axis=0)
    z = _bdot(h, wa_ref[0, :, ATTN_Q + 2 * ATTN_KV:2 * ATTN_Q + 2 * ATTN_KV])
    mix = o * _silu(z)
    xq = _bdot(h, wa_ref[0, :, 2 * ATTN_Q + 2 * ATTN_KV:2 * ATTN_Q + 2 * ATTN_KV + X_WIDTH])
    xz = _bdot(h, wa_ref[0, :, 2 * ATTN_Q + 2 * ATTN_KV + X_WIDTH:ATTN_IN])
    xo = jnp.concatenate(
        [_mem_attn(xq[s * t:(s + 1) * t], _mem_slab(mk_ref, 1, s), _mem_slab(mv_ref, 1, s))
         for s in range(nb)], axis=0) * _silu(xz)
    x = x + _bdot(jnp.concatenate([mix.astype(BF16), xo.astype(BF16)], axis=1), wo_ref[1, :, 0:D_MODEL])

    y_ref[...] = _rms(x, fg_ref[...]).reshape(nb, t, D_MODEL)

    if carry:
        uext[:, 0:HIST_PAD, :] = uext[:, t:t + HIST_PAD, :]
        for buf in (ka, kb, va, vb):
            buf[:, :, 0:WINDOW, :] = buf[:, :, t:t + WINDOW, :]


def _trunk_call(x, cos, sin, hist, ck, cv, mk, mv, v_block, sinks, ng, wc, cw, cb, lg, lb, wa, wo, fg,
                *, nb, t, carry):
    n_streams, seq, _ = x.shape
    if carry:
        assert nb == 1 and seq % t == 0 and t >= WINDOW and t % CHUNK == 0
        grid = (n_streams, seq // t)
        per_stream = lambda b, i: b
        per_tile = lambda b, i: i
    else:
        assert seq == t and n_streams % nb == 0 and t % CHUNK == 0
        grid = (n_streams // nb,)
        per_stream = lambda g: g
        per_tile = lambda g: 0

    def stream_tile(*idx):
        return (per_stream(*idx), per_tile(*idx), 0)

    def stream4(*idx):
        return (0, per_stream(*idx), 0, 0)

    def tile2(*idx):
        return (0, per_tile(*idx))

    pos_block = -(-t // LANES) * LANES
    assert cos.shape == sin.shape == (HEAD_DIM // 2, pos_block * (seq // t))

    def const(rank):
        return lambda *idx: (0,) * rank

    def resident(arr):
        return pl.BlockSpec(arr.shape, const(arr.ndim), pipeline_mode=pl.Buffered(1))

    def small(arr):
        return pl.BlockSpec(arr.shape, const(arr.ndim))

    def mem_spec(arr, last_block):
        if arr.shape[-1] == XHEAD_DIM:
            return pl.BlockSpec((DEPTH, nb, N_MEM * N_XH, XHEAD_DIM), lambda *idx: (0, per_stream(*idx), 0, 0))
        return pl.BlockSpec((DEPTH, nb, N_MEM, X_WIDTH), lambda *idx: (0, per_stream(*idx), 0, last_block),
                            pipeline_mode=pl.Buffered(1))

    in_specs = [
        pl.BlockSpec(memory_space=pltpu.SMEM),
        pl.BlockSpec((nb, t, D_MODEL), stream_tile),
        pl.BlockSpec((HEAD_DIM // 2, pos_block), tile2),
        pl.BlockSpec((HEAD_DIM // 2, pos_block), tile2),
        pl.BlockSpec((1, nb, HIST, CONV_CH), stream4),
        pl.BlockSpec((1, nb, WINDOW, ATTN_KV), stream4),
        pl.BlockSpec((1, nb, WINDOW, ATTN_KV), stream4),
        mem_spec(mk, 0), mem_spec(mv, v_block),
        small(ng), resident(wc), small(cw), small(cb), small(lg), small(lb),
        resident(wa), resident(wo), small(fg),
    ]
    out_shape = (
        jax.ShapeDtypeStruct((n_streams, seq, D_MODEL), F32),
        jax.ShapeDtypeStruct((1, n_streams, HIST, CONV_CH), F32),
        jax.ShapeDtypeStruct((1, n_streams, WINDOW, ATTN_KV), F32),
        jax.ShapeDtypeStruct((1, n_streams, WINDOW, ATTN_KV), F32),
    )
    out_specs = (
        pl.BlockSpec((nb, t, D_MODEL), stream_tile),
        pl.BlockSpec((1, nb, HIST, CONV_CH), stream4),
        pl.BlockSpec((1, nb, WINDOW, ATTN_KV), stream4),
        pl.BlockSpec((1, nb, WINDOW, ATTN_KV), stream4),
    )
    kv_buf = pltpu.VMEM((N_KV, nb, WINDOW + t, ATTN_KV), BF16)
    scratch = [pltpu.VMEM((nb, HIST_PAD + t, CONV_CH), F32), kv_buf, kv_buf, kv_buf, kv_buf]
    return pl.pallas_call(
        functools.partial(_trunk_kernel, nb=nb, t=t, carry=carry),
        grid=grid, in_specs=in_specs, out_specs=out_specs, out_shape=out_shape,
        scratch_shapes=scratch,
        compiler_params=pltpu.CompilerParams(
            dimension_semantics=("arbitrary",) * len(grid), vmem_limit_bytes=VMEM_LIMIT_BYTES),
        name="trunk_prompt" if carry else "trunk_sample",
    )(sinks, x, cos, sin, hist, ck, cv, mk, mv, ng, wc, cw, cb, lg, lb, wa, wo, fg)


def _mem_kv_kernel(mem_ref, w_ref, k_ref, v_ref, kv16_ref):
    kv = _bdot(mem_ref[...], w_ref[0])
    kv16_ref[0] = kv.astype(BF16)
    for h in range(N_XH):
        k_ref[0, :, h, :] = kv[:, h * XHEAD_DIM:(h + 1) * XHEAD_DIM]
        v_ref[0, :, h, :] = kv[:, X_WIDTH + h * XHEAD_DIM:X_WIDTH + (h + 1) * XHEAD_DIM]


def _mem_kv(mem, w):
    rows = mem.shape[0]
    head_shape = (DEPTH, rows, N_XH, XHEAD_DIM)
    head_spec = pl.BlockSpec((1, rows, N_XH, XHEAD_DIM), lambda l: (l, 0, 0, 0))
    return pl.pallas_call(
        _mem_kv_kernel,
        grid=(DEPTH,),
        in_specs=[pl.BlockSpec((rows, D_MODEL), lambda l: (0, 0)),
                  pl.BlockSpec((1, D_MODEL, 2 * X_WIDTH), lambda l: (l, 0, 0))],
        out_specs=(head_spec, head_spec, pl.BlockSpec((1, rows, 2 * X_WIDTH), lambda l: (l, 0, 0))),
        out_shape=(jax.ShapeDtypeStruct(head_shape, F32), jax.ShapeDtypeStruct(head_shape, F32),
                   jax.ShapeDtypeStruct((DEPTH, rows, 2 * X_WIDTH), BF16)),
        compiler_params=pltpu.CompilerParams(dimension_semantics=("arbitrary",)),
        name="mem_kv",
    )(mem, w)


def _bf16_weight(w):
    if (w.shape[-1] // LANES) % SUBLANES == 0:
        w = jnp.pad(w, [(0, 0)] * (w.ndim - 1) + [(0, LANES)])
    return w.astype(BF16)


def _rope_tables(pos):
    half = HEAD_DIM // 2
    inv = ROPE_THETA ** (-jnp.arange(half, dtype=F32) / half)
    ang = inv[:, None] * pos.astype(F32)[None, :]
    return jnp.cos(ang), jnp.sin(ang)


PROMPT_TILE = 256
SAMPLE_STREAMS = 2


def kernel(x_prompt, x_sample, state_conv, cache_swa_k, cache_swa_v, cache_mem_k, cache_mem_v, mem_prompt, norm_g, w_in_conv, conv_w, conv_b, ln_g, ln_b, w_in_attn, sinks, w_mem_kv, w_out, final_g):
    batch, seq, _ = x_prompt.shape
    dec_batch, dec_seq, _ = x_sample.shape
    wc = _bf16_weight(w_in_conv)
    wa = _bf16_weight(w_in_attn)
    wo = _bf16_weight(w_out)
    fg = final_g.reshape(1, D_MODEL)
    sinks1 = sinks.reshape(N_Q)
    weights = (sinks1, norm_g, wc, conv_w, conv_b, ln_g, ln_b, wa, wo, fg)

    mem_k_p, mem_v_p, kv16 = _mem_kv(mem_prompt.reshape(batch * N_MEM, D_MODEL).astype(BF16),
                                     w_mem_kv.astype(BF16))
    kv16 = kv16.reshape(DEPTH, batch, N_MEM, 2 * X_WIDTH)
    mem_k_p = mem_k_p.reshape(DEPTH, batch, N_MEM, N_XH, XHEAD_DIM)
    mem_v_p = mem_v_p.reshape(DEPTH, batch, N_MEM, N_XH, XHEAD_DIM)

    cos_p, sin_p = _rope_tables(jnp.arange(seq, dtype=jnp.int32))
    y_p, conv_p, k_p, v_p = _trunk_call(
        x_prompt, cos_p, sin_p,
        jnp.zeros((1, batch, HIST, CONV_CH), F32),
        jnp.zeros((1, batch, WINDOW, ATTN_KV), F32), jnp.zeros((1, batch, WINDOW, ATTN_KV), F32),
        kv16, kv16, 1, *weights,
        nb=1, t=min(PROMPT_TILE, seq), carry=True)

    cos_s, sin_s = _rope_tables(PAST_LEN + jnp.arange(-(-dec_seq // LANES) * LANES, dtype=jnp.int32))
    y_s, conv_s, k_s, v_s = _trunk_call(
        x_sample, cos_s, sin_s, state_conv,
        cache_swa_k.reshape(1, dec_batch, WINDOW, ATTN_KV), cache_swa_v.reshape(1, dec_batch, WINDOW, ATTN_KV),
        cache_mem_k.reshape(DEPTH, dec_batch, N_MEM * N_XH, XHEAD_DIM),
        cache_mem_v.reshape(DEPTH, dec_batch, N_MEM * N_XH, XHEAD_DIM),
        0, *weights, nb=SAMPLE_STREAMS, t=dec_seq, carry=False)

    kv_shape = lambda n: (1, n, WINDOW, N_KV, HEAD_DIM)
    return (y_p, y_s, conv_p, conv_s,
            k_p.reshape(kv_shape(batch)), v_p.reshape(kv_shape(batch)),
            k_s.reshape(kv_shape(dec_batch)), v_s.reshape(kv_shape(dec_batch)),
            mem_k_p, mem_v_p)
```
